```python
import math
import jax, jax.numpy as jnp
from jax import lax
import numpy as np

D_MODEL = 2048
BATCH = 2
SEQ = 4096
DEPTH = 2
DEC_BATCH = 32
DEC_SEQ = 4
PAST_LEN = 8192
PAGE_SIZE = 128

RWKV_WIDTH = D_MODEL // 2
RWKV_HEAD_DIM = 64
RWKV_HEADS = RWKV_WIDTH // RWKV_HEAD_DIM
DECAY_LORA = 64
ICLR_LORA = 64
GATE_LORA = 160
RWKV_PROJ = 3 * RWKV_WIDTH + DECAY_LORA + ICLR_LORA + GATE_LORA
GN_EPS = 64e-5
MOBA_WIDTH = D_MODEL // 2
MOBA_HEAD_DIM = 128
MOBA_HEADS = MOBA_WIDTH // MOBA_HEAD_DIM
MOBA_BLOCK = 256
MOBA_TOPK = 3
Q_CHUNK = 64
NUM_BUCKETS = 32
MAX_DISTANCE = 2048
W_IN = RWKV_PROJ + 3 * MOBA_WIDTH + 2 * D_MODEL
D_FF = -(-(8 * D_MODEL) // (3 * 256)) * 256
RMS_EPS = 1e-6

kernel_name = 'hybrid_rwkv7_moba_decoder_step'


def rms_norm(x, gain):
    xf = x.astype(jnp.float32)
    y = xf * lax.rsqrt(jnp.mean(xf * xf, axis=-1, keepdims=True) + RMS_EPS)
    return (y * gain.astype(jnp.float32)).astype(x.dtype)


def t5_bucket(dist):
    n = jnp.maximum(dist, 0)
    max_exact = NUM_BUCKETS // 2
    nf = jnp.maximum(n, 1).astype(jnp.float32)
    large = max_exact + (jnp.log(nf / max_exact) / math.log(MAX_DISTANCE / max_exact)
                         * (NUM_BUCKETS - max_exact)).astype(jnp.int32)
    large = jnp.minimum(large, NUM_BUCKETS - 1)
    return jnp.where(n < max_exact, n, large)


def rwkv7_branch(z, shift0, S0, lp):
    B, T, _ = z.shape
    f32 = jnp.float32
    z_prev = jnp.concatenate([shift0[:, None, :].astype(z.dtype), z[:, :-1]], axis=1)
    zm = z + lp['mu_shift'] * (z_prev - z)
    splits = [RWKV_WIDTH, 2 * RWKV_WIDTH, 3 * RWKV_WIDTH, 3 * RWKV_WIDTH + DECAY_LORA,
              3 * RWKV_WIDTH + DECAY_LORA + ICLR_LORA]
    r, k, v, zw, za, zg = jnp.split(zm, splits, axis=-1)
    w_log = -jax.nn.softplus(-(lp['w0'] + jnp.tanh(zw) @ lp['w_lora']).astype(f32)) - 0.5
    decay = jnp.exp(-jnp.exp(w_log))
    a = jax.nn.sigmoid((lp['a0'] + za @ lp['a_lora']).astype(f32))
    g = jax.nn.sigmoid(zg) @ lp['g_lora']

    def heads(t):
        return t.astype(f32).reshape(B, T, RWKV_HEADS, RWKV_HEAD_DIM)

    r_h, v_h, w_h, a_h = heads(r), heads(v), heads(decay), heads(a)
    kk = heads(k * lp['k_k'])
    kk = kk / jnp.maximum(jnp.sqrt(jnp.sum(kk * kk, axis=-1, keepdims=True)), 1e-12)
    k_a = lp['k_a'].astype(f32).reshape(RWKV_HEADS, RWKV_HEAD_DIM)
    k_h = heads(k) * (1.0 + (a_h - 1.0) * k_a)

    def step(S, inp):
        r_t, k_t, v_t, w_t, kk_t, b_t = inp
        S = (S * w_t[:, :, None, :]
             - jnp.einsum('bhvk,bhk->bhv', S, kk_t)[..., None] * b_t[:, :, None, :]
             + v_t[..., None] * k_t[:, :, None, :])
        return S, jnp.einsum('bhvk,bhk->bhv', S, r_t)

    def tmaj(t):
        return jnp.swapaxes(t, 0, 1)

    S_fin, y = lax.scan(step, S0.astype(f32),
                        (tmaj(r_h), tmaj(k_h), tmaj(v_h), tmaj(w_h), tmaj(kk), tmaj(kk * a_h)))
    y = tmaj(y)
    mu = jnp.mean(y, axis=-1, keepdims=True)
    var = jnp.mean(jnp.square(y - mu), axis=-1, keepdims=True)
    y = ((y - mu) * lax.rsqrt(var + GN_EPS)).reshape(B, T, RWKV_WIDTH)
    y = y * lp['lnx_w'].astype(f32) + lp['lnx_b'].astype(f32)
    bonus = jnp.sum(r_h * k_h * lp['r_k'].astype(f32), axis=-1, keepdims=True) * v_h
    y = (y + bonus.reshape(B, T, RWKV_WIDTH)) * g.astype(f32)
    return y.astype(z.dtype), S_fin, z[:, -1]


def moba_blocks(k_all, v_all):
    B, L, H, Dh = k_all.shape
    n_blocks = -(-L // MOBA_BLOCK)
    pad = n_blocks * MOBA_BLOCK - L
    kb = jnp.pad(k_all, ((0, 0), (0, pad), (0, 0), (0, 0))).reshape(B, n_blocks, MOBA_BLOCK, H, Dh)
    vb = jnp.pad(v_all, ((0, 0), (0, pad), (0, 0), (0, 0))).reshape(B, n_blocks, MOBA_BLOCK, H, Dh)
    k_mean = jnp.mean(kb.astype(jnp.float32), axis=2)
    return kb, vb, k_mean


def moba_query_block(q, q_pos, kb, vb, k_mean, rel_bias):
    B, Tq, H, Dh = q.shape
    n_blocks = kb.shape[1]
    topk = min(MOBA_TOPK, n_blocks)
    f32 = jnp.float32
    qf = q.astype(f32)
    own = q_pos // MOBA_BLOCK
    gate = jnp.einsum('bthd,bnhd->bhtn', qf, k_mean)
    past = jnp.arange(n_blocks)[None, :] < own[:, None]
    gate = jnp.where(past[None, None], gate, -jnp.inf)
    _, sel = lax.top_k(gate, topk)
    sel_ok = jnp.arange(topk)[None, :] < jnp.minimum(own, topk)[:, None]
    blk = jnp.concatenate([sel, jnp.broadcast_to(own[None, None, :, None], (B, H, Tq, 1))], axis=-1)
    bi = jnp.arange(B)[:, None, None, None]
    hi = jnp.arange(H)[None, :, None, None]
    k_sel = kb[bi, blk, :, hi, :]
    v_sel = vb[bi, blk, :, hi, :]
    logits = jnp.einsum('bthd,bhtjsd->bhtjs', qf, k_sel.astype(f32)) * (Dh ** -0.5)
    k_pos = blk[..., None] * MOBA_BLOCK + jnp.arange(MOBA_BLOCK)
    dist = q_pos[None, None, :, None, None] - k_pos
    bias = rel_bias[t5_bucket(dist), hi[..., None]].astype(f32)
    ok = jnp.concatenate([sel_ok, jnp.ones((Tq, 1), bool)], axis=-1)[None, None, :, :, None] & (dist >= 0)
    logits = jnp.where(ok, logits + bias, -jnp.inf)
    p = jax.nn.softmax(logits.reshape(B, H, Tq, -1), axis=-1).reshape(logits.shape)
    out = jnp.einsum('bhtjs,bhtjsd->bthd', p, v_sel.astype(f32))
    return out.astype(q.dtype)


def moba_attention(q, k_all, v_all, q_start, rel_bias):
    B, T, H, Dh = q.shape
    kb, vb, k_mean = moba_blocks(k_all, v_all)
    q_pos = q_start + jnp.arange(T, dtype=jnp.int32)
    if T > Q_CHUNK and T % Q_CHUNK == 0:
        nq = T // Q_CHUNK
        qs = jnp.swapaxes(q.reshape(B, nq, Q_CHUNK, H, Dh), 0, 1)
        ps = q_pos.reshape(nq, Q_CHUNK)
        out = lax.map(lambda a: moba_query_block(a[0], a[1], kb, vb, k_mean, rel_bias), (qs, ps))
        return jnp.swapaxes(out, 0, 1).reshape(B, T, H, Dh)
    return moba_query_block(q, q_pos, kb, vb, k_mean, rel_bias)


def trunk_layer(x, c, lp, rel_bias, shift0, S0, k_past, v_past):
    B, T, _ = x.shape
    mod = jax.nn.silu(c) @ lp['w_ada'] + lp['b_ada']
    sh1, sc1, g1, sh2, sc2, g2 = [m[:, None, :] for m in jnp.split(mod, 6, axis=-1)]
    h = rms_norm(x, lp['norm_mix']) * (1.0 + sc1) + sh1
    z = h @ lp['w_in']
    offs = [RWKV_PROJ, RWKV_PROJ + MOBA_WIDTH, RWKV_PROJ + 2 * MOBA_WIDTH,
            RWKV_PROJ + 3 * MOBA_WIDTH, RWKV_PROJ + 3 * MOBA_WIDTH + D_MODEL]
    z_a, q, k, v, gz_a, gz_b = jnp.split(z, offs, axis=-1)
    y_a, S_new, shift_new = rwkv7_branch(z_a, shift0, S0, lp)

    def heads_b(t):
        return t.reshape(B, T, MOBA_HEADS, MOBA_HEAD_DIM)

    q = rms_norm(heads_b(q), lp['q_norm'])
    k = rms_norm(heads_b(k), lp['k_norm'])
    v = heads_b(v)
    if k_past is None:
        k_all, v_all, q_start = k, v, 0
    else:
        k_all = jnp.concatenate([k_past, k], axis=1)
        v_all = jnp.concatenate([v_past, v], axis=1)
        q_start = k_past.shape[1]
    y_b = moba_attention(q, k_all, v_all, q_start, rel_bias).reshape(B, T, MOBA_WIDTH)
    merged = jax.nn.sigmoid(gz_a) * (y_a @ lp['w_up_a']) + jax.nn.sigmoid(gz_b) * (y_b @ lp['w_up_b'])
    x = x + g1 * (merged @ lp['w_out'])
    h2 = rms_norm(x, lp['norm_ffn']) * (1.0 + sc2) + sh2
    x = x + g2 * ((jax.nn.silu(h2 @ lp['w_ffn_gate']) * (h2 @ lp['w_ffn_up'])) @ lp['w_ffn_down'])
    return x, k, v, S_new, shift_new


def setup_inputs(seed: int = 0) -> dict:
    key = jax.random.key(seed)
    ks = iter(jax.random.split(key, 48))
    f32 = jnp.float32

    def nrm(shape, scale):
        return scale * jax.random.normal(next(ks), shape, f32)

    def unif(shape, lo, hi):
        return jax.random.uniform(next(ks), shape, f32, lo, hi)

    n_pages = PAST_LEN // PAGE_SIZE
    n_used = DEC_BATCH * n_pages
    n_pool = n_used + max(1, n_used // 4)
    page_table = jax.random.permutation(next(ks), n_pool)[:n_used].reshape(DEC_BATCH, n_pages).astype(jnp.int32)
    L = DEPTH
    return {
        'x_prompt': nrm((BATCH, SEQ, D_MODEL), 1.0),
        'x_sample': nrm((DEC_BATCH, DEC_SEQ, D_MODEL), 1.0),
        'cache_k': nrm((L, n_pool, PAGE_SIZE, MOBA_HEADS, MOBA_HEAD_DIM), 1.0),
        'cache_v': nrm((L, n_pool, PAGE_SIZE, MOBA_HEADS, MOBA_HEAD_DIM), 1.0),
        'state_wkv': nrm((L, DEC_BATCH, RWKV_HEADS, RWKV_HEAD_DIM, RWKV_HEAD_DIM), 0.3),
        'state_shift': nrm((L, DEC_BATCH, RWKV_PROJ), 1.0),
        'page_table': page_table,
        'c_prompt': nrm((BATCH, D_MODEL), 1.0),
        'c_sample': nrm((DEC_BATCH, D_MODEL), 1.0),
        'rel_bias': nrm((NUM_BUCKETS, MOBA_HEADS), 0.5),
        'w_ada': nrm((L, D_MODEL, 6 * D_MODEL), D_MODEL ** -0.5),
        'b_ada': nrm((L, 6 * D_MODEL), 0.02),
        'norm_mix': 1.0 + nrm((L, D_MODEL), 0.02),
        'norm_ffn': 1.0 + nrm((L, D_MODEL), 0.02),
        'w_in': nrm((L, D_MODEL, W_IN), D_MODEL ** -0.5),
        'mu_shift': unif((L, RWKV_PROJ), 0.0, 1.0),
        'w0': unif((L, RWKV_WIDTH), -6.0, -1.0),
        'w_lora': nrm((L, DECAY_LORA, RWKV_WIDTH), 0.1),
        'a0': nrm((L, RWKV_WIDTH), 0.3),
        'a_lora': nrm((L, ICLR_LORA, RWKV_WIDTH), 0.1),
        'g_lora': nrm((L, GATE_LORA, RWKV_WIDTH), GATE_LORA ** -0.5),
        'k_k': 0.85 + nrm((L, RWKV_WIDTH), 0.05),
        'k_a': 1.0 + nrm((L, RWKV_WIDTH), 0.05),
        'r_k': nrm((L, RWKV_HEADS, RWKV_HEAD_DIM), 0.1),
        'lnx_w': 1.0 + nrm((L, RWKV_WIDTH), 0.02),
        'lnx_b': nrm((L, RWKV_WIDTH), 0.02),
        'q_norm': 1.0 + nrm((L, MOBA_HEAD_DIM), 0.02),
        'k_norm': 1.0 + nrm((L, MOBA_HEAD_DIM), 0.02),
        'w_up_a': nrm((L, RWKV_WIDTH, D_MODEL), RWKV_WIDTH ** -0.5),
        'w_up_b': nrm((L, MOBA_WIDTH, D_MODEL), MOBA_WIDTH ** -0.5),
        'w_out': nrm((L, D_MODEL, D_MODEL), D_MODEL ** -0.5),
        'w_ffn_gate': nrm((L, D_MODEL, D_FF), D_MODEL ** -0.5),
        'w_ffn_up': nrm((L, D_MODEL, D_FF), D_MODEL ** -0.5),
        'w_ffn_down': nrm((L, D_FF, D_MODEL), D_FF ** -0.5),
    }


def reference(x_prompt, x_sample, cache_k, cache_v, state_wkv, state_shift, page_table, c_prompt, c_sample,
              rel_bias, w_ada, b_ada, norm_mix, norm_ffn, w_in, mu_shift, w0, w_lora, a0, a_lora, g_lora,
              k_k, k_a, r_k, lnx_w, lnx_b, q_norm, k_norm, w_up_a, w_up_b, w_out, w_ffn_gate, w_ffn_up,
              w_ffn_down):
    y_p, y_s = x_prompt, x_sample
    bp = x_prompt.shape[0]
    bs = x_sample.shape[0]
    kp_l, vp_l, sp_l, shp_l, ks_l, vs_l, ss_l, shs_l = [], [], [], [], [], [], [], []
    for l in range(DEPTH):
        lp = {'w_ada': w_ada[l], 'b_ada': b_ada[l], 'norm_mix': norm_mix[l], 'norm_ffn': norm_ffn[l],
              'w_in': w_in[l], 'mu_shift': mu_shift[l], 'w0': w0[l], 'w_lora': w_lora[l], 'a0': a0[l],
              'a_lora': a_lora[l], 'g_lora': g_lora[l], 'k_k': k_k[l], 'k_a': k_a[l], 'r_k': r_k[l],
              'lnx_w': lnx_w[l], 'lnx_b': lnx_b[l], 'q_norm': q_norm[l], 'k_norm': k_norm[l],
              'w_up_a': w_up_a[l], 'w_up_b': w_up_b[l], 'w_out': w_out[l], 'w_ffn_gate': w_ffn_gate[l],
              'w_ffn_up': w_ffn_up[l], 'w_ffn_down': w_ffn_down[l]}
        shift0_p = jnp.zeros((bp, RWKV_PROJ), x_prompt.dtype)
        S0_p = jnp.zeros((bp, RWKV_HEADS, RWKV_HEAD_DIM, RWKV_HEAD_DIM), jnp.float32)
        y_p, k_p, v_p, S_p, sh_p = trunk_layer(y_p, c_prompt, lp, rel_bias, shift0_p, S0_p, None, None)
        k_past = cache_k[l][page_table].reshape(bs, -1, MOBA_HEADS, MOBA_HEAD_DIM)
        v_past = cache_v[l][page_table].reshape(bs, -1, MOBA_HEADS, MOBA_HEAD_DIM)
        y_s, k_s, v_s, S_s, sh_s = trunk_layer(y_s, c_sample, lp, rel_bias, state_shift[l], state_wkv[l],
                                               k_past, v_past)
        kp_l.append(k_p); vp_l.append(v_p); sp_l.append(S_p); shp_l.append(sh_p)
        ks_l.append(k_s); vs_l.append(v_s); ss_l.append(S_s); shs_l.append(sh_s)
    k_new_p = jnp.stack(kp_l)
    v_new_p = jnp.stack(vp_l)
    wkv_p = jnp.stack(sp_l)
    shift_p = jnp.stack(shp_l)
    k_new_s = jnp.stack(ks_l)
    v_new_s = jnp.stack(vs_l)
    wkv_s = jnp.stack(ss_l)
    shift_s = jnp.stack(shs_l)
    return (y_p, y_s, k_new_p, v_new_p, wkv_p, shift_p, k_new_s, v_new_s, wkv_s, shift_s)
```

```python
import functools
import math

import jax
import jax.numpy as jnp
import numpy as np
from jax import lax
from jax.experimental import pallas as pl
from jax.experimental.pallas import tpu as pltpu

F32 = jnp.float32
BF16 = jnp.bfloat16

LANES = 128
SUBLANES = 8
VMEM_LIMIT_BYTES = 56 * 1024 * 1024

D_MODEL = 2048
DEPTH = 2
PAGE_SIZE = 128
RWKV_WIDTH = 1024
RWKV_HEAD_DIM = 64
RWKV_HEADS = 16
RWKV_PAIRS = RWKV_HEADS // 2
DECAY_LORA = 64
ICLR_LORA = 64
GATE_LORA = 160
RWKV_PROJ = 3 * RWKV_WIDTH + DECAY_LORA + ICLR_LORA + GATE_LORA
RWKV_PROJ_PAD = 3456
LORA_OFF = 3 * RWKV_WIDTH
GN_EPS = 64e-5
MOBA_WIDTH = 1024
MOBA_HEAD_DIM = 128
MOBA_HEADS = 8
MOBA_BLOCK = 256
MOBA_TOPK = 3
NUM_BUCKETS = 32
MAX_DISTANCE = 2048
D_FF = 5632
RMS_EPS = 1e-6
NEG_BIG = -1e30


def _cparams(*sem):
    return pltpu.CompilerParams(dimension_semantics=sem, vmem_limit_bytes=VMEM_LIMIT_BYTES)


def _dot(a, b):
    return jnp.dot(a, b, preferred_element_type=F32)


def _dot_nt(a, b):
    return lax.dot_general(a, b, (((1,), (1,)), ((), ())), preferred_element_type=F32)


def _dot_tn(a, b):
    return lax.dot_general(a, b, (((0,), (0,)), ((), ())), preferred_element_type=F32)


def _split2(x):
    hi = x.astype(BF16)
    lo = (x - hi.astype(F32)).astype(BF16)
    return hi, lo


def _split3(x):
    hi = x.astype(BF16)
    r1 = x - hi.astype(F32)
    mid = r1.astype(BF16)
    lo = (r1 - mid.astype(F32)).astype(BF16)
    return hi, mid, lo


def _rwkv_kernel(z_ref, shift0_ref, s0_ref, mu_ref, vecs_ref, ww_ref, wa_ref, wg_ref, seg_ref, segt_ref,
                 y_ref, sfin_ref, shift_ref, carry_z, state, *, chunk, t_valid):
    c = chunk
    ci = pl.program_id(1)
    n_chunks = pl.num_programs(1)

    @pl.when(ci == 0)
    def _():
        carry_z[...] = shift0_ref[0]
        state[...] = s0_ref[0]

    z = z_ref[0]
    row = lax.broadcasted_iota(jnp.int32, (c, 1), 0)
    z_prev = jnp.where(row == 0, carry_z[...], pltpu.roll(z, 1, axis=0))
    carry_z[...] = z[c - 1:c, :]
    last_row = (t_valid - 1) % c

    @pl.when(ci == n_chunks - 1)
    def _():
        shift_ref[0] = z[last_row:last_row + 1, :]

    zm = z + mu_ref[...] * (z_prev - z)
    r = zm[:, 0:RWKV_WIDTH]
    k = zm[:, RWKV_WIDTH:2 * RWKV_WIDTH]
    v = zm[:, 2 * RWKV_WIDTH:3 * RWKV_WIDTH]
    l0 = zm[:, LORA_OFF:LORA_OFF + LANES]
    l1 = zm[:, LORA_OFF + LANES:RWKV_PROJ_PAD]

    w0 = vecs_ref[0:1, :]
    a0 = vecs_ref[1:2, :]
    k_k = vecs_ref[2:3, :]
    k_a = vecs_ref[3:4, :]
    r_k = vecs_ref[4:5, :]
    lnx_w = vecs_ref[5:6, :]
    lnx_b = vecs_ref[6:7, :]

    wl = w0 + _dot(jnp.tanh(l0).astype(BF16), ww_ref[...])
    neg = -wl
    softplus = jnp.maximum(neg, 0.0) + jnp.log(1.0 + jnp.exp(-jnp.abs(neg)))
    logw = -jnp.exp(-softplus - 0.5)
    a = jax.nn.sigmoid(a0 + _dot(l0.astype(BF16), wa_ref[...]))
    g = _dot(jax.nn.sigmoid(l1).astype(BF16), wg_ref[...])

    seg = seg_ref[...]
    segt = segt_ref[...]

    def head_sum(x):
        hi, lo = _split2(x)
        s = _dot(hi, seg) + _dot(lo, seg)
        shi, slo = _split2(s)
        return _dot(shi, segt) + _dot(slo, segt)

    kk = k * k_k
    kk = kk / jnp.maximum(jnp.sqrt(head_sum(kk * kk)), 1e-12)
    k_h = k * (1.0 + (a - 1.0) * k_a)
    if t_valid % c != 0:
        live = (ci * c + row) < t_valid
        logw = jnp.where(live, logw, 0.0)
        kk = jnp.where(live, kk, 0.0)
        k_h = jnp.where(live, k_h, 0.0)
    b = kk * a

    ri = lax.broadcasted_iota(jnp.int32, (c, c), 0)
    cj = lax.broadcasted_iota(jnp.int32, (c, c), 1)
    tri = jnp.where(ri >= cj, 1.0, 0.0).astype(BF16)
    h1, h2, h3 = _split3(logw)
    lc = _dot(tri, h1) + _dot(tri, h2) + _dot(tri, h3)
    lc_end = lc[c - 1:c, :]
    inv_g = jnp.exp(-lc)
    r_hat = r * jnp.exp(lc)
    kap_hat = kk * jnp.exp(lc - logw)
    k_hat = k_h * inv_g
    b_hat = b * inv_g
    to_end = jnp.exp(lc_end - lc)
    k_bar = k_h * to_end
    b_bar = b * to_end
    g_end = jnp.exp(lc_end)

    c2 = 2 * c
    lane = lax.broadcasted_iota(jnp.int32, (1, LANES), 1)
    in_a = lane < RWKV_HEAD_DIM

    def stack(x):
        return jnp.concatenate([jnp.where(in_a, x, 0.0), jnp.where(in_a, 0.0, x)], axis=0).astype(BF16)

    i2 = lax.broadcasted_iota(jnp.int32, (c2, c2), 0)
    j2 = lax.broadcasted_iota(jnp.int32, (c2, c2), 1)
    strict = i2 > j2
    lower = i2 >= j2
    eye = jnp.where(i2 == j2, 1.0, 0.0)
    ys = []
    for p in range(RWKV_PAIRS):
        sl = slice(p * LANES, (p + 1) * LANES)
        kap2, r2 = stack(kap_hat[:, sl]), stack(r_hat[:, sl])
        kh2, bh2 = stack(k_hat[:, sl]), stack(b_hat[:, sl])
        kb2, bb2 = stack(k_bar[:, sl]), stack(b_bar[:, sl])
        v2 = stack(v[:, sl])
        s_p = state[p]
        s_bf = s_p.astype(BF16)
        nn = jnp.where(strict, _dot_nt(kap2, bh2), 0.0)
        mm = jnp.where(strict, _dot_nt(kap2, kh2), 0.0)
        ark = jnp.where(lower, _dot_nt(r2, kh2), 0.0)
        arb = jnp.where(lower, _dot_nt(r2, bh2), 0.0)
        rhs = -(_dot_nt(kap2, s_bf) + _dot(mm.astype(BF16), v2))
        t_inv = eye
        s = 1
        while s < c:
            sel = (((i2 ^ j2) & (-2 * s)) == 0) & ((i2 & s) != 0) & ((j2 & s) == 0)
            off = jnp.where(sel, nn, 0.0).astype(BF16)
            t_bf = t_inv.astype(BF16)
            t_inv = t_inv - _dot(_dot(t_bf, off).astype(BF16), t_bf)
            s *= 2
        e2 = _dot(t_inv.astype(BF16), rhs.astype(BF16)).astype(BF16)
        y2 = _dot_nt(r2, s_bf) + _dot(ark.astype(BF16), v2) + _dot(arb.astype(BF16), e2)
        ys.append(y2[0:c, :] + y2[c:c2, :])
        state[p] = s_p * g_end[:, sl] + _dot_tn(v2, kb2) + _dot_tn(e2, bb2)
    y = jnp.concatenate(ys, axis=1)

    inv_n = 1.0 / RWKV_HEAD_DIM
    mean = head_sum(y) * inv_n
    yc = y - mean
    var = head_sum(yc * yc) * inv_n
    yn = yc * lax.rsqrt(var + GN_EPS) * lnx_w + lnx_b
    bonus = head_sum(r * k_h * r_k) * v
    y_ref[0] = ((yn + bonus) * g).astype(y_ref.dtype)

    @pl.when(ci == n_chunks - 1)
    def _():
        sfin_ref[0] = state[...]


def _rwkv_branch(z, shift0, s0, lw, *, chunk, t_valid):
    g_, t_, pr = z.shape
    nc = t_ // chunk
    kern = functools.partial(_rwkv_kernel, chunk=chunk, t_valid=t_valid)
    const2 = lambda gi, ci: (0, 0)
    return pl.pallas_call(
        kern,
        grid=(g_, nc),
        in_specs=[
            pl.BlockSpec((1, chunk, pr), lambda gi, ci: (gi, ci, 0)),
            pl.BlockSpec((1, 1, pr), lambda gi, ci: (gi, 0, 0)),
            pl.BlockSpec((1, RWKV_PAIRS, LANES, LANES), lambda gi, ci: (gi, 0, 0, 0)),
            pl.BlockSpec((1, pr), const2),
            pl.BlockSpec((SUBLANES, RWKV_WIDTH), const2),
            pl.BlockSpec((LANES, RWKV_WIDTH), const2),
            pl.BlockSpec((LANES, RWKV_WIDTH), const2),
            pl.BlockSpec((2 * LANES, RWKV_WIDTH), const2),
            pl.BlockSpec((RWKV_WIDTH, LANES), const2),
            pl.BlockSpec((LANES, RWKV_WIDTH), const2),
        ],
        out_specs=[
            pl.BlockSpec((1, chunk, RWKV_WIDTH), lambda gi, ci: (gi, ci, 0)),
            pl.BlockSpec((1, RWKV_PAIRS, LANES, LANES), lambda gi, ci: (gi, 0, 0, 0)),
            pl.BlockSpec((1, 1, pr), lambda gi, ci: (gi, 0, 0)),
        ],
        out_shape=[
            jax.ShapeDtypeStruct((g_, t_, RWKV_WIDTH), BF16),
            jax.ShapeDtypeStruct((g_, RWKV_PAIRS, LANES, LANES), F32),
            jax.ShapeDtypeStruct((g_, 1, pr), F32),
        ],
        scratch_shapes=[pltpu.VMEM((1, pr), F32), pltpu.VMEM((RWKV_PAIRS, LANES, LANES), F32)],
        compiler_params=_cparams("parallel", "arbitrary"),
        name="rwkv7_chunk",
    )(z, shift0, s0, lw["mu"], lw["vecs"], lw["ww"], lw["wa"], lw["wg"], lw["seg"], lw["segt"])


def _rwkv_weights(l, mu_shift, w0, w_lora, a0, a_lora, g_lora, k_k, k_a, r_k, lnx_w, lnx_b):
    pad = RWKV_PROJ_PAD - RWKV_PROJ
    vecs = jnp.stack([w0[l], a0[l], k_k[l], k_a[l], r_k[l].reshape(-1), lnx_w[l], lnx_b[l],
                      jnp.zeros((RWKV_WIDTH,), F32)])
    zeros64 = jnp.zeros((DECAY_LORA, RWKV_WIDTH), F32)
    head_of_lane = np.arange(RWKV_WIDTH) // RWKV_HEAD_DIM
    seg = (head_of_lane[:, None] == np.arange(LANES)[None, :]).astype(np.float32)
    return {
        "mu": jnp.pad(mu_shift[l], (0, pad))[None, :],
        "vecs": vecs,
        "ww": jnp.concatenate([w_lora[l], zeros64]).astype(BF16),
        "wa": jnp.concatenate([zeros64, a_lora[l]]).astype(BF16),
        "wg": jnp.pad(g_lora[l], ((0, 2 * LANES - GATE_LORA), (0, 0))).astype(BF16),
        "seg": jnp.asarray(seg, BF16),
        "segt": jnp.asarray(seg.T, BF16),
    }


def _pair_state(s):
    g_ = s.shape[0]
    s = s.reshape(g_, RWKV_PAIRS, 2, RWKV_HEAD_DIM, RWKV_HEAD_DIM)
    out = jnp.zeros((g_, RWKV_PAIRS, 2, RWKV_HEAD_DIM, 2, RWKV_HEAD_DIM), F32)
    out = out.at[:, :, 0, :, 0, :].set(s[:, :, 0]).at[:, :, 1, :, 1, :].set(s[:, :, 1])
    return out.reshape(g_, RWKV_PAIRS, LANES, LANES)


def _unpair_state(sp):
    g_ = sp.shape[0]
    sp = sp.reshape(g_, RWKV_PAIRS, 2, RWKV_HEAD_DIM, 2, RWKV_HEAD_DIM)
    s = jnp.stack([sp[:, :, 0, :, 0, :], sp[:, :, 1, :, 1, :]], axis=2)
    return s.reshape(g_, RWKV_HEADS, RWKV_HEAD_DIM, RWKV_HEAD_DIM)


def _ada_kernel(c_ref, w_ref, b_ref, o_ref):
    c = c_ref[...]
    act = (c * jax.nn.sigmoid(c)).astype(BF16)
    o_ref[0] = _dot(act, w_ref[0].astype(BF16)) + b_ref[0]


def _ada_mod(c_all, w_ada, b_ada, *, tn=1024):
    r_, d = c_all.shape
    l_, _, n = w_ada.shape
    return pl.pallas_call(
        _ada_kernel,
        grid=(l_, n // tn),
        in_specs=[
            pl.BlockSpec((r_, d), lambda li, j: (0, 0)),
            pl.BlockSpec((1, d, tn), lambda li, j: (li, 0, j)),
            pl.BlockSpec((1, 1, tn), lambda li, j: (li, 0, j)),
        ],
        out_specs=pl.BlockSpec((1, r_, tn), lambda li, j: (li, 0, j)),
        out_shape=jax.ShapeDtypeStruct((l_, r_, n), F32),
        compiler_params=_cparams("parallel", "parallel"),
        name="ada_mod",
    )(c_all, w_ada, b_ada.reshape(l_, 1, n))


def _mod_spec(mod, chunk, width, tt):
    per_token = mod.shape[1] != 1
    rows = tt if per_token else 1
    blocks_per_chunk = D_MODEL // width

    def index(gi, ti, *rest):
        j = rest[0] if rest else 0
        return (gi, ti if per_token else 0, chunk * blocks_per_chunk + j)

    return pl.BlockSpec((1, rows, width), index)


def _norm_kernel(x_ref, gain_ref, sc_ref, sh_ref, o_ref):
    x = x_ref[0]
    y = x * lax.rsqrt(jnp.mean(x * x, axis=-1, keepdims=True) + RMS_EPS) * gain_ref[...]
    o_ref[0] = (y * (1.0 + sc_ref[0]) + sh_ref[0]).astype(o_ref.dtype)


def _norm_mod(x, gain, mod, sh_chunk, sc_chunk, *, tt):
    g_, t_, d = x.shape
    return pl.pallas_call(
        _norm_kernel,
        grid=(g_, t_ // tt),
        in_specs=[
            pl.BlockSpec((1, tt, d), lambda gi, ti: (gi, ti, 0)),
            pl.BlockSpec((1, d), lambda gi, ti: (0, 0)),
            _mod_spec(mod, sc_chunk, d, tt),
            _mod_spec(mod, sh_chunk, d, tt),
        ],
        out_specs=pl.BlockSpec((1, tt, d), lambda gi, ti: (gi, ti, 0)),
        out_shape=jax.ShapeDtypeStruct((g_, t_, d), BF16),
        compiler_params=_cparams("parallel", "parallel"),
        name="norm_mod",
    )(x, gain.reshape(1, d), mod, mod)


def _matmul(name, lhs, ws, pairs, epilogue, out_dtypes, *, tm, tn, tiles=(), mods=(), cols=()):
    g_, t_, _ = lhs[0].shape
    n = ws[0].shape[1]
    nl, nw, nt, nm, nc = len(lhs), len(ws), len(tiles), len(mods), len(cols)

    def body(*refs):
        lhs_refs = refs[:nl]
        w_refs = refs[nl:nl + nw]
        tile_refs = refs[nl + nw:nl + nw + nt]
        mod_refs = refs[nl + nw + nt:nl + nw + nt + nm]
        col_refs = refs[nl + nw + nt + nm:nl + nw + nt + nm + nc]
        out_refs = refs[nl + nw + nt + nm + nc:]
        dots = [_dot(lhs_refs[pi][0], w_refs[i][...]) for i, pi in enumerate(pairs)]
        outs = epilogue(dots, [r[0] for r in tile_refs], [r[0] for r in mod_refs], [r[...] for r in col_refs])
        for o_ref, o in zip(out_refs, outs):
            o_ref[0] = o.astype(o_ref.dtype)

    in_specs = [pl.BlockSpec((1, tm, a.shape[2]), lambda gi, ti, j: (gi, ti, 0)) for a in lhs]
    in_specs += [pl.BlockSpec((w.shape[0], tn), lambda gi, ti, j: (0, j)) for w in ws]
    in_specs += [pl.BlockSpec((1, tm, tn), lambda gi, ti, j: (gi, ti, j)) for _ in tiles]
    in_specs += [_mod_spec(m, chunk, tn, tm) for m, chunk in mods]
    in_specs += [pl.BlockSpec((1, tn), lambda gi, ti, j: (0, j)) for _ in cols]
    out = pl.pallas_call(
        body,
        grid=(g_, t_ // tm, n // tn),
        in_specs=in_specs,
        out_specs=[pl.BlockSpec((1, tm, tn), lambda gi, ti, j: (gi, ti, j)) for _ in out_dtypes],
        out_shape=[jax.ShapeDtypeStruct((g_, t_, n), dt) for dt in out_dtypes],
        compiler_params=_cparams("parallel", "parallel", "arbitrary"),
        name=name,
    )(*lhs, *ws, *tiles, *[m for m, _ in mods], *cols)
    return out


def _ep_plain(dots, tiles, mods, cols):
    return [dots[0]]


def _ep_plain2(dots, tiles, mods, cols):
    return [dots[0], dots[0]]


def _ep_head_norm(dots, tiles, mods, cols):
    x = dots[0]
    parts = []
    for j in range(x.shape[1] // MOBA_HEAD_DIM):
        xs = x[:, j * MOBA_HEAD_DIM:(j + 1) * MOBA_HEAD_DIM]
        parts.append(xs * lax.rsqrt(jnp.mean(xs * xs, axis=-1, keepdims=True) + RMS_EPS))
    y = jnp.concatenate(parts, axis=1) * cols[0]
    return [y, y]


def _ep_merge(dots, tiles, mods, cols):
    return [jax.nn.sigmoid(dots[0]) * dots[2] + jax.nn.sigmoid(dots[1]) * dots[3]]


def _ep_residual(dots, tiles, mods, cols):
    return [tiles[0] + mods[0] * dots[0]]


def _ep_swiglu(dots, tiles, mods, cols):
    gate = dots[0]
    return [gate * jax.nn.sigmoid(gate) * dots[1]]


def _t5_bucket(dist):
    n = jnp.maximum(dist, 0)
    max_exact = NUM_BUCKETS // 2
    nf = jnp.maximum(n, 1).astype(F32)
    large = max_exact + (jnp.log(nf / max_exact) / math.log(MAX_DISTANCE / max_exact)
                         * (NUM_BUCKETS - max_exact)).astype(jnp.int32)
    large = jnp.minimum(large, NUM_BUCKETS - 1)
    return jnp.where(n < max_exact, n, large)


def _top_blocks(gates, lane_f):
    sel = jnp.zeros_like(gates)
    for _ in range(MOBA_TOPK):
        mx = jnp.max(gates, axis=-1, keepdims=True)
        first = jnp.min(jnp.where(gates == mx, lane_f, float(LANES)), axis=-1, keepdims=True)
        pick = (lane_f == first) & (mx > 0.5 * NEG_BIG)
        sel = jnp.where(pick, 1.0, sel)
        gates = jnp.where(pick, NEG_BIG, gates)
    return sel


def _kmean_kernel(k_ref, o_ref):
    n = pl.program_id(1)
    o_ref[0, pl.ds(n, 1), :] = jnp.mean(k_ref[0], axis=0, keepdims=True)


def _block_means(qk):
    b_, t_, _ = qk.shape
    nb = t_ // MOBA_BLOCK
    return pl.pallas_call(
        _kmean_kernel,
        grid=(b_, nb),
        in_specs=[pl.BlockSpec((1, MOBA_BLOCK, MOBA_WIDTH), lambda bi, n: (bi, n, 1))],
        out_specs=pl.BlockSpec((1, nb, MOBA_WIDTH), lambda bi, n: (bi, 0, 0)),
        out_shape=jax.ShapeDtypeStruct((b_, nb, MOBA_WIDTH), F32),
        compiler_params=_cparams("parallel", "arbitrary"),
        name="moba_block_means",
    )(qk)


def _moba_prompt_kernel(ii_ref, jj_ref, q_ref, k_ref, v_ref, km_ref, rt_ref, o_ref,
                        m_sc, l_sc, acc_sc, sel_sc, *, n_blocks):
    step = pl.program_id(2)
    i = ii_ref[step]
    j = jj_ref[step]
    blk = MOBA_BLOCK
    q = q_ref[0]
    lane_f = lax.broadcasted_iota(jnp.int32, (1, LANES), 1).astype(F32)

    @pl.when(j == 0)
    def _():
        m_sc[...] = jnp.full(m_sc.shape, NEG_BIG, F32)
        l_sc[...] = jnp.zeros(l_sc.shape, F32)
        acc_sc[...] = jnp.zeros(acc_sc.shape, F32)
        km = km_ref[0]
        gates = jnp.full((blk, LANES), NEG_BIG, F32)
        for n in range(n_blocks - 1):
            g_n = jnp.sum(q * km[n:n + 1, :], axis=-1, keepdims=True)
            gates = jnp.where(lane_f == float(n), jnp.where(n < i, g_n, NEG_BIG), gates)
        sel_sc[...] = _top_blocks(gates, lane_f)

    s = _dot_nt(q.astype(BF16), k_ref[0]) * (MOBA_HEAD_DIM ** -0.5)
    start = pl.multiple_of((n_blocks - 1 - (i - j)) * blk, blk)
    window = rt_ref[0, :, pl.ds(start, 2 * blk)]
    rolled = pltpu.roll(jnp.broadcast_to(window, (blk, 2 * blk)), 0, 1, stride=1, stride_axis=0)
    bias = rolled[:, blk:2 * blk]
    lane_i = lax.broadcasted_iota(jnp.int32, (1, LANES), 1)
    picked = jnp.max(jnp.where(lane_i == j, sel_sc[...], 0.0), axis=-1, keepdims=True)
    row_ok = jnp.where(i == j, 1.0, picked)
    r_i = lax.broadcasted_iota(jnp.int32, (blk, blk), 0)
    c_i = lax.broadcasted_iota(jnp.int32, (blk, blk), 1)
    on_diag = (i == j).astype(jnp.int32)
    ok = ((c_i - r_i) * on_diag <= 0) & (row_ok > 0.0)
    logits = jnp.where(ok, s + bias, NEG_BIG)
    m_prev = m_sc[...]
    m_new = jnp.maximum(m_prev, jnp.max(logits, axis=-1, keepdims=True))
    p = jnp.where(ok, jnp.exp(logits - m_new), 0.0)
    alpha = jnp.exp(m_prev - m_new)
    l_sc[...] = alpha * l_sc[...] + jnp.sum(p, axis=-1, keepdims=True)
    acc_sc[...] = alpha * acc_sc[...] + _dot(p.astype(BF16), v_ref[0])
    m_sc[...] = m_new

    @pl.when(j == i)
    def _():
        o_ref[0] = (acc_sc[...] / l_sc[...]).astype(o_ref.dtype)


def _moba_prompt(qk, qk_bf, v_bf, kmean, rev_bias):
    b_, t_, _ = qk.shape
    nb = t_ // MOBA_BLOCK
    pairs = [(i, j) for i in range(nb) for j in range(i + 1)]
    ii = jnp.asarray([p[0] for p in pairs], jnp.int32)
    jj = jnp.asarray([p[1] for p in pairs], jnp.int32)
    h_ = MOBA_HEADS
    blk = MOBA_BLOCK
    grid_spec = pltpu.PrefetchScalarGridSpec(
        num_scalar_prefetch=2,
        grid=(b_, h_, len(pairs)),
        in_specs=[
            pl.BlockSpec((1, blk, LANES), lambda bi, h, s, ii_r, jj_r: (bi, ii_r[s], h)),
            pl.BlockSpec((1, blk, LANES), lambda bi, h, s, ii_r, jj_r: (bi, jj_r[s], h_ + h)),
            pl.BlockSpec((1, blk, LANES), lambda bi, h, s, ii_r, jj_r: (bi, jj_r[s], h)),
            pl.BlockSpec((1, nb, LANES), lambda bi, h, s, ii_r, jj_r: (bi, 0, h)),
            pl.BlockSpec((1, 1, rev_bias.shape[2]), lambda bi, h, s, ii_r, jj_r: (h, 0, 0)),
        ],
        out_specs=pl.BlockSpec((1, blk, LANES), lambda bi, h, s, ii_r, jj_r: (bi, ii_r[s], h)),
        scratch_shapes=[pltpu.VMEM((blk, 1), F32), pltpu.VMEM((blk, 1), F32),
                        pltpu.VMEM((blk, LANES), F32), pltpu.VMEM((blk, LANES), F32)],
    )
    return pl.pallas_call(
        functools.partial(_moba_prompt_kernel, n_blocks=nb),
        grid_spec=grid_spec,
        out_shape=jax.ShapeDtypeStruct((b_, t_, MOBA_WIDTH), BF16),
        compiler_params=_cparams("parallel", "parallel", "arbitrary"),
        name="moba_prompt",
    )(ii, jj, qk, qk_bf, v_bf, kmean, rev_bias)


def _prompt_bias_table(rel_bias, t_):
    y = jnp.arange(t_ + MOBA_BLOCK, dtype=jnp.int32)
    tab = rel_bias[_t5_bucket(t_ - y)]
    tab = jnp.where((y <= t_)[:, None], tab, 0.0)
    return tab.T.reshape(MOBA_HEADS, 1, t_ + MOBA_BLOCK)


SAMPLE_ROWS = 16


def _moba_sample_kernel(pt_ref, q_ref, k0_ref, k1_ref, v0_ref, v1_ref, kn_ref, vn_ref, bias_ref, ownb_ref,
                        o_ref, m_sc, l_sc, g_sc, acc_sc, *, n_past):
    n = pl.program_id(1)
    scale = MOBA_HEAD_DIM ** -0.5
    q = q_ref[0]
    start = pl.multiple_of(n * MOBA_BLOCK, MOBA_BLOCK)
    for h in range(MOBA_HEADS):
        hs = slice(h * LANES, (h + 1) * LANES)
        q_h = q[:, hs]
        kf = jnp.concatenate([k0_ref[:, h, :], k1_ref[:, h, :]], axis=0)
        vf = jnp.concatenate([v0_ref[:, h, :], v1_ref[:, h, :]], axis=0)
        logits = _dot_nt(q_h.astype(BF16), kf.astype(BF16)) * scale + bias_ref[h, :, pl.ds(start, MOBA_BLOCK)]
        m_b = jnp.max(logits, axis=-1, keepdims=True)
        p = jnp.exp(logits - m_b)
        shape = (SAMPLE_ROWS, LANES)
        m_sc[n, h] = jnp.broadcast_to(m_b, shape)
        l_sc[n, h] = jnp.broadcast_to(jnp.sum(p, axis=-1, keepdims=True), shape)
        acc_sc[n, h] = _dot(p.astype(BF16), vf.astype(BF16))
        kmean = jnp.mean(kf, axis=0, keepdims=True)
        g_sc[n, h] = jnp.broadcast_to(jnp.sum(q_h * kmean, axis=-1, keepdims=True), shape)

    @pl.when(n == n_past - 1)
    def _():
        lane_f = lax.broadcasted_iota(jnp.int32, (1, LANES), 1).astype(F32)
        for h in range(MOBA_HEADS):
            hs = slice(h * LANES, (h + 1) * LANES)
            q_h = q[:, hs]
            gates = jnp.full((SAMPLE_ROWS, LANES), NEG_BIG, F32)
            for nb in range(n_past):
                gates = jnp.where(lane_f == float(nb), g_sc[nb, h], gates)
            sel = _top_blocks(gates, lane_f)
            s_own = _dot_nt(q_h.astype(BF16), kn_ref[0][:, hs].astype(BF16)) * scale + ownb_ref[h]
            m_tot = jnp.broadcast_to(jnp.max(s_own, axis=-1, keepdims=True), (SAMPLE_ROWS, LANES))
            picked = []
            for nb in range(n_past):
                on = jnp.max(jnp.where(lane_f == float(nb), sel, 0.0), axis=-1, keepdims=True) > 0.0
                picked.append(on)
                m_tot = jnp.maximum(m_tot, jnp.where(on, m_sc[nb, h], NEG_BIG))
            p_own = jnp.exp(s_own - m_tot[:, 0:SAMPLE_ROWS])
            l_tot = jnp.broadcast_to(jnp.sum(p_own, axis=-1, keepdims=True), (SAMPLE_ROWS, LANES))
            acc = _dot(p_own.astype(BF16), vn_ref[0][:, hs].astype(BF16))
            for nb in range(n_past):
                w = jnp.where(picked[nb], jnp.exp(m_sc[nb, h] - m_tot), 0.0)
                l_tot = l_tot + w * l_sc[nb, h]
                acc = acc + w * acc_sc[nb, h]
            o_ref[0, :, hs] = (acc / l_tot).astype(o_ref.dtype)


def _moba_sample(layer, page_table, q, k_new, v_new, cache_k, cache_v, bias_tab, own_bias):
    s_ = q.shape[0]
    n_past = page_table.shape[1] * PAGE_SIZE // MOBA_BLOCK
    page = (None, None, PAGE_SIZE, MOBA_HEADS, MOBA_HEAD_DIM)
    row = pl.BlockSpec((1, SAMPLE_ROWS, MOBA_WIDTH), lambda si, n, pt: (si, 0, 0))
    grid_spec = pltpu.PrefetchScalarGridSpec(
        num_scalar_prefetch=1,
        grid=(s_, n_past),
        in_specs=[
            row,
            pl.BlockSpec(page, lambda si, n, pt: (layer, pt[si, 2 * n], 0, 0, 0)),
            pl.BlockSpec(page, lambda si, n, pt: (layer, pt[si, 2 * n + 1], 0, 0, 0)),
            pl.BlockSpec(page, lambda si, n, pt: (layer, pt[si, 2 * n], 0, 0, 0)),
            pl.BlockSpec(page, lambda si, n, pt: (layer, pt[si, 2 * n + 1], 0, 0, 0)),
            row,
            row,
            pl.BlockSpec(bias_tab.shape, lambda si, n, pt: (0, 0, 0)),
            pl.BlockSpec(own_bias.shape, lambda si, n, pt: (0, 0, 0)),
        ],
        out_specs=row,
        scratch_shapes=[pltpu.VMEM((n_past, MOBA_HEADS, SAMPLE_ROWS, LANES), F32) for _ in range(4)],
    )
    return pl.pallas_call(
        functools.partial(_moba_sample_kernel, n_past=n_past),
        grid_spec=grid_spec,
        out_shape=jax.ShapeDtypeStruct((s_, SAMPLE_ROWS, MOBA_WIDTH), BF16),
        compiler_params=_cparams("parallel", "arbitrary"),
        name="moba_sample",
    )(page_table, q, cache_k, cache_k, cache_v, cache_v, k_new, v_new, bias_tab, own_bias)


def _sample_bias_tables(rel_bias, past_len, t_new):
    t = jnp.arange(SAMPLE_ROWS, dtype=jnp.int32)
    kpos = jnp.arange(past_len, dtype=jnp.int32)
    tab = rel_bias[_t5_bucket(past_len + t[:, None] - kpos[None, :])]
    tab = jnp.where((t < t_new)[:, None, None], tab, 0.0)
    c = jnp.arange(SAMPLE_ROWS, dtype=jnp.int32)
    own = rel_bias[_t5_bucket(t[:, None] - c[None, :])]
    visible = (c[None, :] <= t[:, None]) & (c[None, :] < t_new) & (t[:, None] < t_new)
    pad_row = (t[:, None] >= t_new) & (c[None, :] == 0)
    own = jnp.where(visible[..., None], own, jnp.where(pad_row[..., None], 0.0, NEG_BIG))
    return jnp.transpose(tab, (2, 0, 1)), jnp.transpose(own, (2, 0, 1))


SH1, SC1, G1, SH2, SC2, G2 = range(6)


def _layer_weights(l, w_in, q_norm, k_norm, w_up_a, w_up_b, w_out, w_ffn_gate, w_ffn_up, w_ffn_down):
    o_q = RWKV_PROJ
    o_v = o_q + 2 * MOBA_WIDTH
    o_ga = o_v + MOBA_WIDTH
    o_gb = o_ga + D_MODEL
    wl = w_in[l]
    heads = MOBA_WIDTH // MOBA_HEAD_DIM
    return {
        "rwkv": jnp.pad(wl[:, :RWKV_PROJ], ((0, 0), (0, RWKV_PROJ_PAD - RWKV_PROJ))).astype(BF16),
        "qk": wl[:, o_q:o_v].astype(BF16),
        "v": wl[:, o_v:o_ga].astype(BF16),
        "ga": wl[:, o_ga:o_gb].astype(BF16),
        "gb": wl[:, o_gb:].astype(BF16),
        "qk_gain": jnp.concatenate([jnp.tile(q_norm[l], heads), jnp.tile(k_norm[l], heads)])[None, :],
        "up_a": w_up_a[l].astype(BF16),
        "up_b": w_up_b[l].astype(BF16),
        "out": w_out[l].astype(BF16),
        "fg": w_ffn_gate[l].astype(BF16),
        "fu": w_ffn_up[l].astype(BF16),
        "fd": w_ffn_down[l].astype(BF16),
    }


def _trunk_layer(x, mod, norm_mix, norm_ffn, w, rwkv_fn, attn_fn, *, tm):
    h = _norm_mod(x, norm_mix, mod, SH1, SC1, tt=min(tm, 512))
    z_r = _matmul("in_rwkv", [h], [w["rwkv"]], [0], _ep_plain, [F32], tm=tm, tn=RWKV_PROJ_PAD // 3)[0]
    qk, qk_bf = _matmul("in_qk", [h], [w["qk"]], [0], _ep_head_norm, [F32, BF16], tm=tm, tn=1024,
                        cols=[w["qk_gain"]])
    v, v_bf = _matmul("in_v", [h], [w["v"]], [0], _ep_plain2, [F32, BF16], tm=tm, tn=1024)
    y_a, s_new, shift_new = rwkv_fn(z_r)
    y_b = attn_fn(qk, qk_bf, v, v_bf)
    merged = _matmul("merge", [h, y_a, y_b], [w["ga"], w["gb"], w["up_a"], w["up_b"]], [0, 0, 1, 2],
                     _ep_merge, [BF16], tm=tm, tn=512)[0]
    x1 = _matmul("out_proj", [merged], [w["out"]], [0], _ep_residual, [F32], tm=tm, tn=512,
                 tiles=[x], mods=[(mod, G1)])[0]
    h2 = _norm_mod(x1, norm_ffn, mod, SH2, SC2, tt=min(tm, 512))
    act = _matmul("ffn_in", [h2], [w["fg"], w["fu"]], [0, 0], _ep_swiglu, [BF16], tm=tm, tn=512)[0]
    x2 = _matmul("ffn_out", [act], [w["fd"]], [0], _ep_residual, [F32], tm=tm, tn=256,
                 tiles=[x1], mods=[(mod, G2)])[0]
    return x2, qk, v, s_new, shift_new


def kernel(x_prompt, x_sample, cache_k, cache_v, state_wkv, state_shift, page_table, c_prompt, c_sample, rel_bias, w_ada, b_ada, norm_mix, norm_ffn, w_in, mu_shift, w0, w_lora, a0, a_lora, g_lora, k_k, k_a, r_k, lnx_w, lnx_b, q_norm, k_norm, w_up_a, w_up_b, w_out, w_ffn_gate, w_ffn_up, w_ffn_down):
    bp, tp, d = x_prompt.shape
    bs, ts, _ = x_sample.shape
    n_seq = bp + bs
    c_all = jnp.concatenate([c_prompt, c_sample])
    c_all = jnp.pad(c_all, ((0, -n_seq % SUBLANES), (0, 0)))
    mod = _ada_mod(c_all, w_ada, b_ada)

    past_len = page_table.shape[1] * PAGE_SIZE
    rev_bias = _prompt_bias_table(rel_bias, tp)
    bias_tab, own_bias = _sample_bias_tables(rel_bias, past_len, ts)
    pr_pad = RWKV_PROJ_PAD - RWKV_PROJ
    sample_chunk = SUBLANES

    def sample_rows(a):
        return jnp.pad(a.reshape(bs, ts, a.shape[-1]), ((0, 0), (0, SAMPLE_ROWS - ts), (0, 0)))

    x_p = x_prompt
    x_s = x_sample.reshape(1, bs * ts, d)
    outs = [[] for _ in range(8)]
    for l in range(DEPTH):
        w = _layer_weights(l, w_in, q_norm, k_norm, w_up_a, w_up_b, w_out, w_ffn_gate, w_ffn_up, w_ffn_down)
        lw = _rwkv_weights(l, mu_shift, w0, w_lora, a0, a_lora, g_lora, k_k, k_a, r_k, lnx_w, lnx_b)
        mod_p = mod[l, :bp][:, None, :]
        mod_s = jnp.repeat(mod[l, bp:n_seq], ts, axis=0)[None]

        def rwkv_p(z_r):
            shift0 = jnp.zeros((bp, 1, RWKV_PROJ_PAD), F32)
            s0 = jnp.zeros((bp, RWKV_PAIRS, LANES, LANES), F32)
            return _rwkv_branch(z_r, shift0, s0, lw, chunk=64, t_valid=tp)

        def attn_p(qk, qk_bf, v, v_bf):
            return _moba_prompt(qk, qk_bf, v_bf, _block_means(qk), rev_bias)

        x_p, qk_p, v_p, s_p, sh_p = _trunk_layer(x_p, mod_p, norm_mix[l], norm_ffn[l], w, rwkv_p, attn_p, tm=1024)

        def rwkv_s(z_r):
            z = z_r.reshape(bs, ts, RWKV_PROJ_PAD)
            z = jnp.pad(z, ((0, 0), (0, sample_chunk - ts), (0, 0)))
            shift0 = jnp.pad(state_shift[l], ((0, 0), (0, pr_pad)))[:, None, :]
            y, s_new, shift_new = _rwkv_branch(z, shift0, _pair_state(state_wkv[l]), lw,
                                               chunk=sample_chunk, t_valid=ts)
            return y[:, :ts].reshape(1, bs * ts, RWKV_WIDTH), s_new, shift_new

        def attn_s(qk, qk_bf, v, v_bf):
            y = _moba_sample(l, page_table, sample_rows(qk[..., :MOBA_WIDTH]), sample_rows(qk[..., MOBA_WIDTH:]),
                             sample_rows(v), cache_k, cache_v, bias_tab, own_bias)
            return y[:, :ts].reshape(1, bs * ts, MOBA_WIDTH)

        x_s, qk_s, v_s, s_s, sh_s = _trunk_layer(x_s, mod_s, norm_mix[l], norm_ffn[l], w, rwkv_s, attn_s,
                                                 tm=bs * ts)

        kv_shape_p = (bp, tp, MOBA_HEADS, MOBA_HEAD_DIM)
        kv_shape_s = (bs, ts, MOBA_HEADS, MOBA_HEAD_DIM)
        for lst, val in zip(outs, [
                qk_p[..., MOBA_WIDTH:].reshape(kv_shape_p), v_p.reshape(kv_shape_p),
                _unpair_state(s_p), sh_p[:, 0, :RWKV_PROJ],
                qk_s[..., MOBA_WIDTH:].reshape(kv_shape_s), v_s.reshape(kv_shape_s),
                _unpair_state(s_s), sh_s[:, 0, :RWKV_PROJ]]):
            lst.append(val)
    return (x_p, x_s.reshape(bs, ts, d)) + tuple(jnp.stack(o) for o in outs)
```

```python
import functools
import math

import jax
import jax.numpy as jnp
import numpy as np
from jax import lax
from jax.experimental import pallas as pl
from jax.experimental.pallas import tpu as pltpu

F32 = jnp.float32
BF16 = jnp.bfloat16

LANES = 128
SUBLANES = 8
VMEM_LIMIT_BYTES = 56 * 1024 * 1024

D_MODEL = 2048
DEPTH = 2
PAGE_SIZE = 128
RWKV_WIDTH = 1024
RWKV_HEAD_DIM = 64
RWKV_HEADS = 16
RWKV_GROUP = 4
DECAY_LORA = 64
ICLR_LORA = 64
GATE_LORA = 160
RWKV_PROJ = 3 * RWKV_WIDTH + DECAY_LORA + ICLR_LORA + GATE_LORA
RWKV_PROJ_PAD = 3456
LORA_OFF = 3 * RWKV_WIDTH
GN_EPS = 64e-5
MOBA_WIDTH = 1024
MOBA_HEAD_DIM = 128
MOBA_HEADS = 8
MOBA_BLOCK = 256
PAGES_PER_BLOCK = MOBA_BLOCK // PAGE_SIZE
MOBA_TOPK = 3
NUM_BUCKETS = 32
MAX_DISTANCE = 2048
D_FF = 5632
RMS_EPS = 1e-6
NEG_BIG = -1e30


def _cparams(*sem):
    return pltpu.CompilerParams(dimension_semantics=sem, vmem_limit_bytes=VMEM_LIMIT_BYTES)


def _dot(a, b):
    return jnp.dot(a, b, preferred_element_type=F32)


def _dot_nt(a, b):
    return lax.dot_general(a, b, (((1,), (1,)), ((), ())), preferred_element_type=F32)


def _dot_tn(a, b):
    return lax.dot_general(a, b, (((0,), (0,)), ((), ())), preferred_element_type=F32)


def _split2(x):
    hi = x.astype(BF16)
    lo = (x - hi.astype(F32)).astype(BF16)
    return hi, lo


def _split3(x):
    hi = x.astype(BF16)
    r1 = x - hi.astype(F32)
    mid = r1.astype(BF16)
    lo = (r1 - mid.astype(F32)).astype(BF16)
    return hi, mid, lo


def _rwkv_kernel(z_ref, shift0_ref, s0_ref, mu_ref, vecs_ref, ww_ref, wa_ref, wg_ref, seg_ref, segt_ref,
                 y_ref, sfin_ref, shift_ref, carry_z, state, *, chunk, t_valid):
    c = chunk
    ci = pl.program_id(1)
    n_chunks = pl.num_programs(1)

    hd = RWKV_HEAD_DIM

    def head_block(h):
        grp, pos = divmod(h, RWKV_GROUP)
        return grp, slice(pos * hd, (pos + 1) * hd)

    @pl.when(ci == 0)
    def _():
        carry_z[...] = shift0_ref[0]
        state[...] = jnp.zeros(state.shape, F32)
        for h in range(RWKV_HEADS):
            grp, sl = head_block(h)
            state[grp, sl, sl] = s0_ref[0, h]

    z = z_ref[0]
    row = lax.broadcasted_iota(jnp.int32, (c, 1), 0)
    z_prev = jnp.where(row == 0, carry_z[...], pltpu.roll(z, 1, axis=0))
    carry_z[...] = z[c - 1:c, :]
    last_row = (t_valid - 1) % c

    @pl.when(ci == n_chunks - 1)
    def _():
        shift_ref[0] = z[last_row:last_row + 1, :]

    zm = z + mu_ref[...] * (z_prev - z)
    r = zm[:, 0:RWKV_WIDTH]
    k = zm[:, RWKV_WIDTH:2 * RWKV_WIDTH]
    v = zm[:, 2 * RWKV_WIDTH:3 * RWKV_WIDTH]
    l0 = zm[:, LORA_OFF:LORA_OFF + LANES]
    l1 = zm[:, LORA_OFF + LANES:RWKV_PROJ_PAD]

    w0 = vecs_ref[0:1, :]
    a0 = vecs_ref[1:2, :]
    k_k = vecs_ref[2:3, :]
    k_a = vecs_ref[3:4, :]
    r_k = vecs_ref[4:5, :]
    lnx_w = vecs_ref[5:6, :]
    lnx_b = vecs_ref[6:7, :]

    wl = w0 + _dot(jnp.tanh(l0).astype(BF16), ww_ref[...])
    neg = -wl
    softplus = jnp.maximum(neg, 0.0) + jnp.log(1.0 + jnp.exp(-jnp.abs(neg)))
    logw = -jnp.exp(-softplus - 0.5)
    a = jax.nn.sigmoid(a0 + _dot(l0.astype(BF16), wa_ref[...]))
    g = _dot(jax.nn.sigmoid(l1).astype(BF16), wg_ref[...])

    seg = seg_ref[...]
    segt = segt_ref[...]

    def head_sum(x):
        hi, lo = _split2(x)
        s = _dot(hi, seg) + _dot(lo, seg)
        shi, slo = _split2(s)
        return _dot(shi, segt) + _dot(slo, segt)

    kk = k * k_k
    kk = kk / jnp.maximum(jnp.sqrt(head_sum(kk * kk)), 1e-12)
    k_h = k * (1.0 + (a - 1.0) * k_a)
    if t_valid % c != 0:
        live = (ci * c + row) < t_valid
        logw = jnp.where(live, logw, 0.0)
        kk = jnp.where(live, kk, 0.0)
        k_h = jnp.where(live, k_h, 0.0)
    b = kk * a

    ri = lax.broadcasted_iota(jnp.int32, (c, c), 0)
    cj = lax.broadcasted_iota(jnp.int32, (c, c), 1)
    tri = jnp.where(ri >= cj, 1.0, 0.0).astype(BF16)
    h1, h2, h3 = _split3(logw)
    lc = _dot(tri, h1) + _dot(tri, h2) + _dot(tri, h3)
    lc_end = lc[c - 1:c, :]
    inv_g = jnp.exp(-lc)
    r_hat = r * jnp.exp(lc)
    kap_hat = kk * jnp.exp(lc - logw)
    k_hat = k_h * inv_g
    b_hat = b * inv_g
    to_end = jnp.exp(lc_end - lc)
    k_bar = k_h * to_end
    b_bar = b * to_end
    g_end = jnp.exp(lc_end)

    gw = RWKV_GROUP * hd
    cg = RWKV_GROUP * c
    head_of_lane = lax.broadcasted_iota(jnp.int32, (1, gw), 1) // hd

    def stack(x):
        return jnp.concatenate([jnp.where(head_of_lane == j, x, 0.0) for j in range(RWKV_GROUP)],
                               axis=0).astype(BF16)

    ig = lax.broadcasted_iota(jnp.int32, (cg, cg), 0)
    jg = lax.broadcasted_iota(jnp.int32, (cg, cg), 1)
    strict = ig > jg
    lower = ig >= jg
    eye = jnp.where(ig == jg, 1.0, 0.0)

    def level(s):
        return (((ig ^ jg) & (-2 * s)) == 0) & ((ig & s) != 0) & ((jg & s) == 0)

    fused = cg % LANES == 0
    ys = []
    for grp in range(RWKV_HEADS // RWKV_GROUP):
        sl = slice(grp * gw, (grp + 1) * gw)
        kap_s, r_s = stack(kap_hat[:, sl]), stack(r_hat[:, sl])
        kh_s, bh_s = stack(k_hat[:, sl]), stack(b_hat[:, sl])
        kb_s, bb_s = stack(k_bar[:, sl]), stack(b_bar[:, sl])
        v_s = stack(v[:, sl])
        s_g = state[grp]
        s_bf = s_g.astype(BF16)
        lhs = jnp.concatenate([kap_s, r_s], axis=0)
        if fused:
            sc = _dot_nt(lhs, jnp.concatenate([bh_s, kh_s], axis=0))
            nn_raw, mm_raw = sc[0:cg, 0:cg], sc[0:cg, cg:2 * cg]
            arb_raw, ark_raw = sc[cg:2 * cg, 0:cg], sc[cg:2 * cg, cg:2 * cg]
        else:
            nn_raw, mm_raw = _dot_nt(kap_s, bh_s), _dot_nt(kap_s, kh_s)
            arb_raw, ark_raw = _dot_nt(r_s, bh_s), _dot_nt(r_s, kh_s)
        nn = jnp.where(strict, nn_raw, 0.0)
        mm = jnp.where(strict, mm_raw, 0.0)
        ark = jnp.where(lower, ark_raw, 0.0)
        arb = jnp.where(lower, arb_raw, 0.0)
        from_state = _dot_nt(lhs, s_bf)
        from_v = _dot(jnp.concatenate([mm, ark], axis=0).astype(BF16), v_s)
        rhs = -(from_state[0:cg] + from_v[0:cg])
        t_inv = eye - jnp.where(level(1), nn, 0.0)
        s = 2
        while s < c:
            off = jnp.where(level(s), nn, 0.0).astype(BF16)
            t_bf = t_inv.astype(BF16)
            t_inv = t_inv - _dot(_dot(t_bf, off).astype(BF16), t_bf)
            s *= 2
        e_s = _dot(t_inv.astype(BF16), rhs.astype(BF16)).astype(BF16)
        y_s = from_state[cg:2 * cg] + from_v[cg:2 * cg] + _dot(arb.astype(BF16), e_s)
        y_g = y_s[0:c]
        for j in range(1, RWKV_GROUP):
            y_g = y_g + y_s[j * c:(j + 1) * c]
        ys.append(y_g)
        state[grp] = s_g * g_end[:, sl] + _dot_tn(jnp.concatenate([v_s, e_s], axis=0),
                                                  jnp.concatenate([kb_s, bb_s], axis=0))
    y = jnp.concatenate(ys, axis=1)

    inv_n = 1.0 / RWKV_HEAD_DIM
    mean = head_sum(y) * inv_n
    yc = y - mean
    var = head_sum(yc * yc) * inv_n
    yn = yc * lax.rsqrt(var + GN_EPS) * lnx_w + lnx_b
    bonus = head_sum(r * k_h * r_k) * v
    y_ref[0] = ((yn + bonus) * g).astype(y_ref.dtype)

    @pl.when(ci == n_chunks - 1)
    def _():
        for h in range(RWKV_HEADS):
            grp, sl = head_block(h)
            sfin_ref[0, h] = state[grp, sl, sl]


def _rwkv_branch(z, shift0, s0, lw, *, chunk, t_valid):
    g_, t_, pr = z.shape
    nc = t_ // chunk
    kern = functools.partial(_rwkv_kernel, chunk=chunk, t_valid=t_valid)
    const2 = lambda gi, ci: (0, 0)
    state_block = (1, RWKV_HEADS, RWKV_HEAD_DIM, RWKV_HEAD_DIM)
    group_lanes = RWKV_GROUP * RWKV_HEAD_DIM
    return pl.pallas_call(
        kern,
        grid=(g_, nc),
        in_specs=[
            pl.BlockSpec((1, chunk, pr), lambda gi, ci: (gi, ci, 0)),
            pl.BlockSpec((1, 1, pr), lambda gi, ci: (gi, 0, 0)),
            pl.BlockSpec(state_block, lambda gi, ci: (gi, 0, 0, 0)),
            pl.BlockSpec((1, pr), const2),
            pl.BlockSpec((SUBLANES, RWKV_WIDTH), const2),
            pl.BlockSpec((LANES, RWKV_WIDTH), const2),
            pl.BlockSpec((LANES, RWKV_WIDTH), const2),
            pl.BlockSpec((2 * LANES, RWKV_WIDTH), const2),
            pl.BlockSpec((RWKV_WIDTH, LANES), const2),
            pl.BlockSpec((LANES, RWKV_WIDTH), const2),
        ],
        out_specs=[
            pl.BlockSpec((1, chunk, RWKV_WIDTH), lambda gi, ci: (gi, ci, 0)),
            pl.BlockSpec(state_block, lambda gi, ci: (gi, 0, 0, 0)),
            pl.BlockSpec((1, 1, pr), lambda gi, ci: (gi, 0, 0)),
        ],
        out_shape=[
            jax.ShapeDtypeStruct((g_, t_, RWKV_WIDTH), BF16),
            jax.ShapeDtypeStruct((g_,) + state_block[1:], F32),
            jax.ShapeDtypeStruct((g_, 1, pr), F32),
        ],
        scratch_shapes=[pltpu.VMEM((1, pr), F32),
                        pltpu.VMEM((RWKV_HEADS // RWKV_GROUP, group_lanes, group_lanes), F32)],
        compiler_params=_cparams("parallel", "arbitrary"),
        name="rwkv7_chunk",
    )(z, shift0, s0, lw["mu"], lw["vecs"], lw["ww"], lw["wa"], lw["wg"], lw["seg"], lw["segt"])


def _rwkv_weights(l, mu_shift, w0, w_lora, a0, a_lora, g_lora, k_k, k_a, r_k, lnx_w, lnx_b):
    pad = RWKV_PROJ_PAD - RWKV_PROJ
    vecs = jnp.stack([w0[l], a0[l], k_k[l], k_a[l], r_k[l].reshape(-1), lnx_w[l], lnx_b[l],
                      jnp.zeros((RWKV_WIDTH,), F32)])
    zeros64 = jnp.zeros((DECAY_LORA, RWKV_WIDTH), F32)
    head_of_lane = np.arange(RWKV_WIDTH) // RWKV_HEAD_DIM
    seg = (head_of_lane[:, None] == np.arange(LANES)[None, :]).astype(np.float32)
    return {
        "mu": jnp.pad(mu_shift[l], (0, pad))[None, :],
        "vecs": vecs,
        "ww": jnp.concatenate([w_lora[l], zeros64]).astype(BF16),
        "wa": jnp.concatenate([zeros64, a_lora[l]]).astype(BF16),
        "wg": jnp.pad(g_lora[l], ((0, 2 * LANES - GATE_LORA), (0, 0))).astype(BF16),
        "seg": jnp.asarray(seg, BF16),
        "segt": jnp.asarray(seg.T, BF16),
    }


def _ada_kernel(c_ref, w_ref, b_ref, o_ref):
    c = c_ref[...]
    act = (c * jax.nn.sigmoid(c)).astype(BF16)
    o_ref[0] = _dot(act, w_ref[0].astype(BF16)) + b_ref[0]


def _ada_mod(c_all, w_ada, b_ada, *, tn=1024):
    r_, d = c_all.shape
    l_, _, n = w_ada.shape
    return pl.pallas_call(
        _ada_kernel,
        grid=(l_, n // tn),
        in_specs=[
            pl.BlockSpec((r_, d), lambda li, j: (0, 0)),
            pl.BlockSpec((1, d, tn), lambda li, j: (li, 0, j)),
            pl.BlockSpec((1, 1, tn), lambda li, j: (li, 0, j)),
        ],
        out_specs=pl.BlockSpec((1, r_, tn), lambda li, j: (li, 0, j)),
        out_shape=jax.ShapeDtypeStruct((l_, r_, n), F32),
        compiler_params=_cparams("parallel", "parallel"),
        name="ada_mod",
    )(c_all, w_ada, b_ada.reshape(l_, 1, n))


def _mod_spec(mod, chunk, width, tt):
    per_token = mod.shape[1] != 1
    rows = tt if per_token else 1
    blocks_per_chunk = D_MODEL // width

    def index(gi, ti, *rest):
        j = rest[0] if rest else 0
        return (gi, ti if per_token else 0, chunk * blocks_per_chunk + j)

    return pl.BlockSpec((1, rows, width), index)


def _norm_kernel(x_ref, gain_ref, sc_ref, sh_ref, o_ref):
    x = x_ref[0]
    y = x * lax.rsqrt(jnp.mean(x * x, axis=-1, keepdims=True) + RMS_EPS) * gain_ref[...]
    o_ref[0] = (y * (1.0 + sc_ref[0]) + sh_ref[0]).astype(o_ref.dtype)


def _norm_mod(x, gain, mod, sh_chunk, sc_chunk, *, tt):
    g_, t_, d = x.shape
    return pl.pallas_call(
        _norm_kernel,
        grid=(g_, t_ // tt),
        in_specs=[
            pl.BlockSpec((1, tt, d), lambda gi, ti: (gi, ti, 0)),
            pl.BlockSpec((1, d), lambda gi, ti: (0, 0)),
            _mod_spec(mod, sc_chunk, d, tt),
            _mod_spec(mod, sh_chunk, d, tt),
        ],
        out_specs=pl.BlockSpec((1, tt, d), lambda gi, ti: (gi, ti, 0)),
        out_shape=jax.ShapeDtypeStruct((g_, t_, d), BF16),
        compiler_params=_cparams("parallel", "parallel"),
        name="norm_mod",
    )(x, gain.reshape(1, d), mod, mod)


def _matmul(name, lhs, ws, pairs, epilogue, out_dtypes, *, tm, tn, tiles=(), mods=(), cols=()):
    g_, t_, _ = lhs[0].shape
    w_arrays = [w[0] if isinstance(w, tuple) else w for w in ws]
    n = w_arrays[0].shape[-1]
    nl, nw, nt, nm, nc = len(lhs), len(ws), len(tiles), len(mods), len(cols)

    def w_spec(w):
        if isinstance(w, tuple):
            layer = w[1]
            return pl.BlockSpec((None, w[0].shape[1], tn), lambda gi, ti, j: (layer, 0, j))
        return pl.BlockSpec((w.shape[0], tn), lambda gi, ti, j: (0, j))

    def body(*refs):
        lhs_refs = refs[:nl]
        w_refs = refs[nl:nl + nw]
        tile_refs = refs[nl + nw:nl + nw + nt]
        mod_refs = refs[nl + nw + nt:nl + nw + nt + nm]
        col_refs = refs[nl + nw + nt + nm:nl + nw + nt + nm + nc]
        out_refs = refs[nl + nw + nt + nm + nc:]
        dots = [_dot(lhs_refs[pi][0], w_refs[i][...].astype(BF16)) for i, pi in enumerate(pairs)]
        outs = epilogue(dots, [r[0] for r in tile_refs], [r[0] for r in mod_refs], [r[...] for r in col_refs])
        for o_ref, o in zip(out_refs, outs):
            o_ref[0] = o.astype(o_ref.dtype)

    in_specs = [pl.BlockSpec((1, tm, a.shape[2]), lambda gi, ti, j: (gi, ti, 0)) for a in lhs]
    in_specs += [w_spec(w) for w in ws]
    in_specs += [pl.BlockSpec((1, tm, tn), lambda gi, ti, j: (gi, ti, j)) for _ in tiles]
    in_specs += [_mod_spec(m, chunk, tn, tm) for m, chunk in mods]
    in_specs += [pl.BlockSpec((1, tn), lambda gi, ti, j: (0, j)) for _ in cols]
    out = pl.pallas_call(
        body,
        grid=(g_, t_ // tm, n // tn),
        in_specs=in_specs,
        out_specs=[pl.BlockSpec((1, tm, tn), lambda gi, ti, j: (gi, ti, j)) for _ in out_dtypes],
        out_shape=[jax.ShapeDtypeStruct((g_, t_, n), dt) for dt in out_dtypes],
        compiler_params=_cparams("parallel", "parallel", "arbitrary"),
        name=name,
    )(*lhs, *w_arrays, *tiles, *[m for m, _ in mods], *cols)
    return out


def _ep_plain(dots, tiles, mods, cols):
    return [dots[0]]


def _ep_plain2(dots, tiles, mods, cols):
    return [dots[0], dots[0]]


def _ep_head_norm(dots, tiles, mods, cols):
    x = dots[0]
    parts = []
    for j in range(x.shape[1] // MOBA_HEAD_DIM):
        xs = x[:, j * MOBA_HEAD_DIM:(j + 1) * MOBA_HEAD_DIM]
        parts.append(xs * lax.rsqrt(jnp.mean(xs * xs, axis=-1, keepdims=True) + RMS_EPS))
    y = jnp.concatenate(parts, axis=1) * cols[0]
    return [y, y]


def _ep_merge(dots, tiles, mods, cols):
    return [jax.nn.sigmoid(dots[0]) * dots[2] + jax.nn.sigmoid(dots[1]) * dots[3]]


def _ep_residual(dots, tiles, mods, cols):
    return [tiles[0] + mods[0] * dots[0]]


def _ep_swiglu(dots, tiles, mods, cols):
    gate = dots[0]
    return [gate * jax.nn.sigmoid(gate) * dots[1]]


def _t5_bucket(dist):
    n = jnp.maximum(dist, 0)
    max_exact = NUM_BUCKETS // 2
    nf = jnp.maximum(n, 1).astype(F32)
    large = max_exact + (jnp.log(nf / max_exact) / math.log(MAX_DISTANCE / max_exact)
                         * (NUM_BUCKETS - max_exact)).astype(jnp.int32)
    large = jnp.minimum(large, NUM_BUCKETS - 1)
    return jnp.where(n < max_exact, n, large)


def _top_blocks(gates, lane_f):
    sel = jnp.zeros_like(gates)
    for _ in range(MOBA_TOPK):
        mx = jnp.max(gates, axis=-1, keepdims=True)
        first = jnp.min(jnp.where(gates == mx, lane_f, float(LANES)), axis=-1, keepdims=True)
        pick = (lane_f == first) & (mx > 0.5 * NEG_BIG)
        sel = jnp.where(pick, 1.0, sel)
        gates = jnp.where(pick, NEG_BIG, gates)
    return sel


def _kmean_kernel(k_ref, o_ref):
    n = pl.program_id(1)
    o_ref[0, pl.ds(n, 1), :] = jnp.mean(k_ref[0], axis=0, keepdims=True)


def _block_means(qk):
    b_, t_, _ = qk.shape
    nb = t_ // MOBA_BLOCK
    return pl.pallas_call(
        _kmean_kernel,
        grid=(b_, nb),
        in_specs=[pl.BlockSpec((1, MOBA_BLOCK, MOBA_WIDTH), lambda bi, n: (bi, n, 1))],
        out_specs=pl.BlockSpec((1, nb, MOBA_WIDTH), lambda bi, n: (bi, 0, 0)),
        out_shape=jax.ShapeDtypeStruct((b_, nb, MOBA_WIDTH), F32),
        compiler_params=_cparams("parallel", "arbitrary"),
        name="moba_block_means",
    )(qk)


def _bias_tile_kernel(rt_ref, o_ref, *, n_blocks):
    delta = pl.program_id(1)
    blk = MOBA_BLOCK
    start = pl.multiple_of((n_blocks - 1 - delta) * blk, blk)
    window = rt_ref[0, :, pl.ds(start, 2 * blk)]
    rolled = pltpu.roll(jnp.broadcast_to(window, (blk, 2 * blk)), 0, 1, stride=1, stride_axis=0)
    r_i = lax.broadcasted_iota(jnp.int32, (blk, blk), 0)
    c_i = lax.broadcasted_iota(jnp.int32, (blk, blk), 1)
    on_diag = (delta == 0).astype(jnp.int32)
    o_ref[0, 0] = jnp.where((c_i - r_i) * on_diag <= 0, rolled[:, blk:2 * blk], NEG_BIG)


def _bias_tiles(rev_bias, n_blocks):
    h_ = rev_bias.shape[0]
    blk = MOBA_BLOCK
    return pl.pallas_call(
        functools.partial(_bias_tile_kernel, n_blocks=n_blocks),
        grid=(h_, n_blocks),
        in_specs=[pl.BlockSpec((1, 1, rev_bias.shape[2]), lambda h, dl: (h, 0, 0))],
        out_specs=pl.BlockSpec((1, 1, blk, blk), lambda h, dl: (h, dl, 0, 0)),
        out_shape=jax.ShapeDtypeStruct((h_, n_blocks, blk, blk), F32),
        compiler_params=_cparams("parallel", "parallel"),
        name="moba_bias_tiles",
    )(rev_bias)


def _moba_prompt_kernel(q_ref, k_ref, v_ref, km_ref, bias_ref, o_ref, qa_sc, m_sc, l_sc, acc_sc, *, n_blocks):
    i = pl.program_id(2)
    blk = MOBA_BLOCK
    nb = n_blocks
    q = q_ref[0]
    km1, km2, km3 = _split3(km_ref[0])
    q1, q2, q3 = _split3(q)
    g1 = _dot_nt(jnp.concatenate([km1, km2, km3], axis=0), q1)
    g2 = _dot_nt(jnp.concatenate([km1, km2], axis=0), q2)
    g3 = _dot_nt(km1, q3)
    gates = (g1[2 * nb:3 * nb] + g2[nb:2 * nb] + g3) + (g1[nb:2 * nb] + g2[0:nb]) + g1[0:nb]
    n_i = lax.broadcasted_iota(jnp.int32, (nb, blk), 0)
    n_f = n_i.astype(F32)
    gates = jnp.where(n_i < i, gates, NEG_BIG)
    sel = jnp.where(n_i == i, 1.0, 0.0)
    for _ in range(MOBA_TOPK):
        mx = jnp.max(gates, axis=0, keepdims=True)
        first = jnp.min(jnp.where(gates == mx, n_f, float(nb)), axis=0, keepdims=True)
        pick = (n_f == first) & (mx > 0.5 * NEG_BIG)
        sel = jnp.where(pick, 1.0, sel)
        gates = jnp.where(pick, NEG_BIG, gates)
    hidden = jnp.where(sel > 0.0, 0.0, NEG_BIG)
    hidden = jnp.concatenate([hidden, jnp.zeros((LANES - nb, blk), F32)], axis=0).T
    qa_sc[...] = jnp.concatenate([(q * (MOBA_HEAD_DIM ** -0.5)).astype(BF16), hidden.astype(BF16)], axis=1)
    m_sc[...] = jnp.full(m_sc.shape, NEG_BIG, F32)
    l_sc[...] = jnp.zeros(l_sc.shape, F32)
    acc_sc[...] = jnp.zeros(acc_sc.shape, F32)
    lane_i = lax.broadcasted_iota(jnp.int32, (blk, LANES), 1)

    def pair(jj, carry):
        qa = qa_sc[...]
        parts, vs = [], []
        for j in (2 * jj, 2 * jj + 1):
            rows = pl.ds(pl.multiple_of(j * blk, blk), blk)
            k_aug = jnp.concatenate([k_ref[0, rows, :], jnp.where(lane_i == j, 1.0, 0.0).astype(BF16)], axis=1)
            parts.append(_dot_nt(qa, k_aug) + bias_ref[0, jnp.maximum(i - j, 0)])
            vs.append(v_ref[0, rows, :])
        cols = [p_[:, x * LANES:(x + 1) * LANES] for p_ in parts for x in range(blk // LANES)]
        top = cols[0]
        for x in cols[1:]:
            top = jnp.maximum(top, x)
        m_prev = m_sc[...]
        m_new = jnp.maximum(m_prev, jnp.max(top, axis=-1, keepdims=True))
        ps = [jnp.exp(x - m_new) for x in cols]
        tot = ps[0]
        for x in ps[1:]:
            tot = tot + x
        alpha = jnp.exp(m_prev - m_new)
        l_sc[...] = alpha * l_sc[...] + jnp.sum(tot, axis=-1, keepdims=True)
        p_all = jnp.concatenate(ps, axis=1).astype(BF16)
        acc_sc[...] = alpha * acc_sc[...] + _dot(p_all, jnp.concatenate(vs, axis=0))
        m_sc[...] = m_new
        return carry

    lax.fori_loop(0, (i + 2) // 2, pair, 0)
    o_ref[0] = (acc_sc[...] / l_sc[...]).astype(o_ref.dtype)


def _moba_prompt(qk, qk_bf, v_bf, kmean, bias_tiles):
    b_, t_, _ = qk.shape
    nb = t_ // MOBA_BLOCK
    h_ = MOBA_HEADS
    blk = MOBA_BLOCK
    return pl.pallas_call(
        functools.partial(_moba_prompt_kernel, n_blocks=nb),
        grid=(b_, h_, nb),
        in_specs=[
            pl.BlockSpec((1, blk, LANES), lambda bi, h, i: (bi, i, h)),
            pl.BlockSpec((1, t_, LANES), lambda bi, h, i: (bi, 0, h_ + h)),
            pl.BlockSpec((1, t_, LANES), lambda bi, h, i: (bi, 0, h)),
            pl.BlockSpec((1, nb, LANES), lambda bi, h, i: (bi, 0, h)),
            pl.BlockSpec((1, nb, blk, blk), lambda bi, h, i: (h, 0, 0, 0)),
        ],
        out_specs=pl.BlockSpec((1, blk, LANES), lambda bi, h, i: (bi, i, h)),
        out_shape=jax.ShapeDtypeStruct((b_, t_, MOBA_WIDTH), BF16),
        scratch_shapes=[pltpu.VMEM((blk, 2 * LANES), BF16), pltpu.VMEM((blk, LANES), F32),
                        pltpu.VMEM((blk, LANES), F32), pltpu.VMEM((blk, LANES), F32)],
        compiler_params=_cparams("parallel", "parallel", "arbitrary"),
        name="moba_prompt",
    )(qk, qk_bf, v_bf, kmean, bias_tiles)


def _prompt_bias_table(rel_bias, t_):
    y = jnp.arange(t_ + MOBA_BLOCK, dtype=jnp.int32)
    tab = rel_bias[_t5_bucket(t_ - y)]
    tab = jnp.where((y <= t_)[:, None], tab, 0.0)
    return tab.T.reshape(MOBA_HEADS, 1, t_ + MOBA_BLOCK)


def _moba_sample_kernel(pt_ref, q_ref, *refs, n_past):
    k_refs, v_refs = refs[:PAGES_PER_BLOCK], refs[PAGES_PER_BLOCK:2 * PAGES_PER_BLOCK]
    kn_ref, vn_ref, bias_ref, ownb_ref, o_ref, m_sc, l_sc, g_sc, acc_sc = refs[2 * PAGES_PER_BLOCK:]
    n = pl.program_id(1)
    scale = MOBA_HEAD_DIM ** -0.5
    rows_n = q_ref.shape[1]
    cols_n = MOBA_BLOCK * MOBA_HEADS
    q = q_ref[0]
    q_bf = (q * scale).astype(BF16)
    shape = (rows_n, LANES)
    kf = jnp.concatenate([r_[...] for r_ in k_refs], axis=0)
    vf = jnp.concatenate([r_[...] for r_ in v_refs], axis=0)
    logits = _dot_nt(q_bf, kf.astype(BF16)) + bias_ref[:, pl.ds(pl.multiple_of(n * cols_n, cols_n), cols_n)]
    m_b = jnp.max(logits, axis=-1, keepdims=True)
    p = jnp.exp(logits - m_b)
    m_sc[n] = jnp.broadcast_to(m_b, shape)
    l_sc[n] = jnp.broadcast_to(jnp.sum(p, axis=-1, keepdims=True), shape)
    acc_sc[n] = _dot(p.astype(BF16), vf.astype(BF16))
    kmean = jnp.mean(kf.reshape(MOBA_BLOCK, MOBA_HEADS, MOBA_HEAD_DIM), axis=0)
    gate = jnp.concatenate([jnp.sum(q[t * MOBA_HEADS:(t + 1) * MOBA_HEADS] * kmean, axis=-1, keepdims=True)
                            for t in range(rows_n // MOBA_HEADS)], axis=0)
    g_sc[n] = jnp.broadcast_to(gate, shape)

    @pl.when(n == n_past - 1)
    def _():
        lane_f = lax.broadcasted_iota(jnp.int32, (1, LANES), 1).astype(F32)
        gates = jnp.full(shape, NEG_BIG, F32)
        for nb in range(n_past):
            gates = jnp.where(lane_f == float(nb), g_sc[nb], gates)
        sel = _top_blocks(gates, lane_f)
        s_own = _dot_nt(q_bf, kn_ref[0].astype(BF16)) + ownb_ref[...]
        m_tot = jnp.broadcast_to(jnp.max(s_own, axis=-1, keepdims=True), shape)
        picked = []
        for nb in range(n_past):
            on = jnp.max(jnp.where(lane_f == float(nb), sel, 0.0), axis=-1, keepdims=True) > 0.0
            picked.append(on)
            m_tot = jnp.maximum(m_tot, jnp.where(on, m_sc[nb], NEG_BIG))
        p_own = jnp.exp(s_own - m_tot[:, 0:rows_n])
        l_tot = jnp.broadcast_to(jnp.sum(p_own, axis=-1, keepdims=True), shape)
        acc = _dot(p_own.astype(BF16), vn_ref[0].astype(BF16))
        for nb in range(n_past):
            w = jnp.where(picked[nb], jnp.exp(m_sc[nb] - m_tot), 0.0)
            l_tot = l_tot + w * l_sc[nb]
            acc = acc + w * acc_sc[nb]
        o_ref[0] = (acc / l_tot).astype(o_ref.dtype)


def _moba_sample(layer, page_table, q, k_new, v_new, cache_k, cache_v, bias_tab, own_bias):
    s_, rows_n, _ = q.shape
    n_past = page_table.shape[1] * PAGE_SIZE // MOBA_BLOCK
    row = pl.BlockSpec((1, rows_n, MOBA_HEAD_DIM), lambda si, n, pt: (si, 0, 0))

    def page_spec(pg):
        return pl.BlockSpec((None, None, PAGE_SIZE * MOBA_HEADS, MOBA_HEAD_DIM),
                            lambda si, n, pt: (layer, pt[si, PAGES_PER_BLOCK * n + pg], 0, 0))

    tiles = [page_spec(pg) for pg in range(PAGES_PER_BLOCK)]
    rows_shape = cache_k.shape[:2] + (PAGE_SIZE * MOBA_HEADS, MOBA_HEAD_DIM)
    cache_k = cache_k.reshape(rows_shape)
    cache_v = cache_v.reshape(rows_shape)
    grid_spec = pltpu.PrefetchScalarGridSpec(
        num_scalar_prefetch=1,
        grid=(s_, n_past),
        in_specs=[row] + tiles + tiles + [
            row,
            row,
            pl.BlockSpec(bias_tab.shape, lambda si, n, pt: (0, 0)),
            pl.BlockSpec(own_bias.shape, lambda si, n, pt: (0, 0)),
        ],
        out_specs=row,
        scratch_shapes=[pltpu.VMEM((n_past, rows_n, LANES), F32) for _ in range(4)],
    )
    return pl.pallas_call(
        functools.partial(_moba_sample_kernel, n_past=n_past),
        grid_spec=grid_spec,
        out_shape=jax.ShapeDtypeStruct((s_, rows_n, MOBA_HEAD_DIM), BF16),
        compiler_params=_cparams("parallel", "arbitrary"),
        name="moba_sample",
    )(page_table, q, *([cache_k] * len(tiles)), *([cache_v] * len(tiles)), k_new, v_new, bias_tab, own_bias)


def _sample_bias_tables(rel_bias, past_len, t_new):
    h_ = rel_bias.shape[1]
    same_head = jnp.eye(h_, dtype=bool)
    dist = past_len + t_new - 1 - jnp.arange(past_len + t_new, dtype=jnp.int32)
    by_dist_rev = rel_bias[_t5_bucket(dist)]
    tab = jnp.stack([by_dist_rev[t_new - 1 - t:t_new - 1 - t + past_len] for t in range(t_new)])
    tab = jnp.where(same_head[None, :, None, :], jnp.transpose(tab, (0, 2, 1))[..., None], NEG_BIG)
    t = jnp.arange(t_new, dtype=jnp.int32)
    own = rel_bias[_t5_bucket(t[:, None] - t[None, :])]
    own = jnp.where((t[None, :] <= t[:, None])[..., None], own, NEG_BIG)
    own = jnp.where(same_head[None, :, None, :], jnp.transpose(own, (0, 2, 1))[..., None], NEG_BIG)
    return tab.reshape(t_new * h_, past_len * h_), own.reshape(t_new * h_, t_new * h_)


PROMPT_ROW_TILE = 1024

SH1, SC1, G1, SH2, SC2, G2 = range(6)


def _layer_weights(l, w_in, q_norm, k_norm, w_up_a, w_up_b, w_out, w_ffn_gate, w_ffn_up, w_ffn_down):
    o_q = RWKV_PROJ
    o_v = o_q + 2 * MOBA_WIDTH
    o_ga = o_v + MOBA_WIDTH
    o_gb = o_ga + D_MODEL
    wl = w_in[l]
    heads = MOBA_WIDTH // MOBA_HEAD_DIM
    return {
        "rwkv": jnp.pad(wl[:, :RWKV_PROJ], ((0, 0), (0, RWKV_PROJ_PAD - RWKV_PROJ))).astype(BF16),
        "qk": wl[:, o_q:o_v].astype(BF16),
        "v": wl[:, o_v:o_ga].astype(BF16),
        "ga": wl[:, o_ga:o_gb].astype(BF16),
        "gb": wl[:, o_gb:].astype(BF16),
        "qk_gain": jnp.concatenate([jnp.tile(q_norm[l], heads), jnp.tile(k_norm[l], heads)])[None, :],
        "up_a": (w_up_a, l),
        "up_b": (w_up_b, l),
        "out": (w_out, l),
        "fg": (w_ffn_gate, l),
        "fu": (w_ffn_up, l),
        "fd": (w_ffn_down, l),
    }


def _trunk_layer(x, mod, norm_mix, norm_ffn, w, rwkv_fn, attn_fn, *, tm):
    h = _norm_mod(x, norm_mix, mod, SH1, SC1, tt=min(tm, 512))
    z_r = _matmul("in_rwkv", [h], [w["rwkv"]], [0], _ep_plain, [F32], tm=tm, tn=RWKV_PROJ_PAD // 3)[0]
    qk, qk_bf = _matmul("in_qk", [h], [w["qk"]], [0], _ep_head_norm, [F32, BF16], tm=tm, tn=1024,
                        cols=[w["qk_gain"]])
    v, v_bf = _matmul("in_v", [h], [w["v"]], [0], _ep_plain2, [F32, BF16], tm=tm, tn=1024)
    y_a, s_new, shift_new = rwkv_fn(z_r)
    y_b = attn_fn(qk, qk_bf, v, v_bf)
    merged = _matmul("merge", [h, y_a, y_b], [w["ga"], w["gb"], w["up_a"], w["up_b"]], [0, 0, 1, 2],
                     _ep_merge, [BF16], tm=tm, tn=512)[0]
    x1 = _matmul("out_proj", [merged], [w["out"]], [0], _ep_residual, [F32], tm=tm, tn=512,
                 tiles=[x], mods=[(mod, G1)])[0]
    h2 = _norm_mod(x1, norm_ffn, mod, SH2, SC2, tt=min(tm, 512))
    act = _matmul("ffn_in", [h2], [w["fg"], w["fu"]], [0, 0], _ep_swiglu, [BF16], tm=tm, tn=512)[0]
    x2 = _matmul("ffn_out", [act], [w["fd"]], [0], _ep_residual, [F32], tm=tm, tn=256,
                 tiles=[x1], mods=[(mod, G2)])[0]
    return x2, qk, v, s_new, shift_new


def kernel(x_prompt, x_sample, cache_k, cache_v, state_wkv, state_shift, page_table, c_prompt, c_sample, rel_bias, w_ada, b_ada, norm_mix, norm_ffn, w_in, mu_shift, w0, w_lora, a0, a_lora, g_lora, k_k, k_a, r_k, lnx_w, lnx_b, q_norm, k_norm, w_up_a, w_up_b, w_out, w_ffn_gate, w_ffn_up, w_ffn_down):
    bp, tp, d = x_prompt.shape
    bs, ts, _ = x_sample.shape
    n_seq = bp + bs
    c_all = jnp.concatenate([c_prompt, c_sample])
    c_all = jnp.pad(c_all, ((0, -n_seq % SUBLANES), (0, 0)))
    mod = _ada_mod(c_all, w_ada, b_ada)

    past_len = page_table.shape[1] * PAGE_SIZE
    bias_tiles = _bias_tiles(_prompt_bias_table(rel_bias, tp), tp // MOBA_BLOCK)
    bias_tab, own_bias = _sample_bias_tables(rel_bias, past_len, ts)
    pr_pad = RWKV_PROJ_PAD - RWKV_PROJ
    sample_chunk = SUBLANES

    def sample_rows(a):
        return a.reshape(bs, ts * MOBA_HEADS, MOBA_HEAD_DIM)

    x_p = x_prompt
    x_s = x_sample.reshape(1, bs * ts, d)
    outs = [[] for _ in range(8)]
    for l in range(DEPTH):
        w = _layer_weights(l, w_in, q_norm, k_norm, w_up_a, w_up_b, w_out, w_ffn_gate, w_ffn_up, w_ffn_down)
        lw = _rwkv_weights(l, mu_shift, w0, w_lora, a0, a_lora, g_lora, k_k, k_a, r_k, lnx_w, lnx_b)
        mod_p = mod[l, :bp][:, None, :]
        mod_s = jnp.repeat(mod[l, bp:n_seq], ts, axis=0)[None]

        def rwkv_p(z_r):
            shift0 = jnp.zeros((bp, 1, RWKV_PROJ_PAD), F32)
            s0 = jnp.zeros((bp, RWKV_HEADS, RWKV_HEAD_DIM, RWKV_HEAD_DIM), F32)
            return _rwkv_branch(z_r, shift0, s0, lw, chunk=64, t_valid=tp)

        def attn_p(qk, qk_bf, v, v_bf):
            return _moba_prompt(qk, qk_bf, v_bf, _block_means(qk), bias_tiles)

        x_p, qk_p, v_p, s_p, sh_p = _trunk_layer(x_p, mod_p, norm_mix[l], norm_ffn[l], w, rwkv_p, attn_p,
                                                 tm=min(tp, PROMPT_ROW_TILE))

        def rwkv_s(z_r):
            z = z_r.reshape(bs, ts, RWKV_PROJ_PAD)
            z = jnp.pad(z, ((0, 0), (0, sample_chunk - ts), (0, 0)))
            shift0 = jnp.pad(state_shift[l], ((0, 0), (0, pr_pad)))[:, None, :]
            y, s_new, shift_new = _rwkv_branch(z, shift0, state_wkv[l], lw,
                                               chunk=sample_chunk, t_valid=ts)
            return y[:, :ts].reshape(1, bs * ts, RWKV_WIDTH), s_new, shift_new

        def attn_s(qk, qk_bf, v, v_bf):
            y = _moba_sample(l, page_table, sample_rows(qk[..., :MOBA_WIDTH]), sample_rows(qk[..., MOBA_WIDTH:]),
                             sample_rows(v), cache_k, cache_v, bias_tab, own_bias)
            return y.reshape(1, bs * ts, MOBA_WIDTH)

        x_s, qk_s, v_s, s_s, sh_s = _trunk_layer(x_s, mod_s, norm_mix[l], norm_ffn[l], w, rwkv_s, attn_s,
                                                 tm=bs * ts)

        kv_shape_p = (bp, tp, MOBA_HEADS, MOBA_HEAD_DIM)
        kv_shape_s = (bs, ts, MOBA_HEADS, MOBA_HEAD_DIM)
        for lst, val in zip(outs, [
                qk_p[..., MOBA_WIDTH:].reshape(kv_shape_p), v_p.reshape(kv_shape_p),
                s_p, sh_p[:, 0, :RWKV_PROJ],
                qk_s[..., MOBA_WIDTH:].reshape(kv_shape_s), v_s.reshape(kv_shape_s),
                s_s, sh_s[:, 0, :RWKV_PROJ]]):
            lst.append(val)
    return (x_p, x_s.reshape(bs, ts, d)) + tuple(jnp.stack(o) for o in outs)
```

```python
import functools
import math

import jax
import jax.numpy as jnp
import numpy as np
from jax import lax
from jax.experimental import pallas as pl
from jax.experimental.pallas import tpu as pltpu

F32 = jnp.float32
BF16 = jnp.bfloat16

LANES = 128
SUBLANES = 8
VMEM_LIMIT_BYTES = 56 * 1024 * 1024

D_MODEL = 2048
DEPTH = 2
PAGE_SIZE = 128
RWKV_WIDTH = 1024
RWKV_HEAD_DIM = 64
RWKV_HEADS = 16
RWKV_GROUP = 4
DECAY_LORA = 64
ICLR_LORA = 64
GATE_LORA = 160
RWKV_PROJ = 3 * RWKV_WIDTH + DECAY_LORA + ICLR_LORA + GATE_LORA
RWKV_PROJ_PAD = 3456
LORA_OFF = 3 * RWKV_WIDTH
GN_EPS = 64e-5
MOBA_WIDTH = 1024
MOBA_HEAD_DIM = 128
MOBA_HEADS = 8
MOBA_BLOCK = 256
PAGES_PER_BLOCK = MOBA_BLOCK // PAGE_SIZE
SAMPLE_BLOCKS_PER_STEP = 2
MOBA_TOPK = 3
NUM_BUCKETS = 32
MAX_DISTANCE = 2048
D_FF = 5632
RMS_EPS = 1e-6
NEG_BIG = -1e30


def _cparams(*sem):
    return pltpu.CompilerParams(dimension_semantics=sem, vmem_limit_bytes=VMEM_LIMIT_BYTES)


def _dot(a, b):
    return jnp.dot(a, b, preferred_element_type=F32)


def _dot_nt(a, b):
    return lax.dot_general(a, b, (((1,), (1,)), ((), ())), preferred_element_type=F32)


def _dot_tn(a, b):
    return lax.dot_general(a, b, (((0,), (0,)), ((), ())), preferred_element_type=F32)


def _split2(x):
    hi = x.astype(BF16)
    lo = (x - hi.astype(F32)).astype(BF16)
    return hi, lo


def _split3(x):
    hi = x.astype(BF16)
    r1 = x - hi.astype(F32)
    mid = r1.astype(BF16)
    lo = (r1 - mid.astype(F32)).astype(BF16)
    return hi, mid, lo


def _rwkv_kernel(z_ref, shift0_ref, s0_ref, mu_ref, vecs_ref, ww_ref, wa_ref, wg_ref, seg_ref, segt_ref,
                 y_ref, sfin_ref, shift_ref, carry_z, state, *, chunk, t_valid):
    c = chunk
    ci = pl.program_id(1)
    n_chunks = pl.num_programs(1)

    hd = RWKV_HEAD_DIM

    def head_block(h):
        grp, pos = divmod(h, RWKV_GROUP)
        return grp, slice(pos * hd, (pos + 1) * hd)

    @pl.when(ci == 0)
    def _():
        carry_z[...] = shift0_ref[0]
        state[...] = jnp.zeros(state.shape, F32)
        for h in range(RWKV_HEADS):
            grp, sl = head_block(h)
            state[grp, sl, sl] = s0_ref[0, h]

    z = z_ref[0]
    row = lax.broadcasted_iota(jnp.int32, (c, 1), 0)
    z_prev = jnp.where(row == 0, carry_z[...], pltpu.roll(z, 1, axis=0))
    carry_z[...] = z[c - 1:c, :]
    last_row = (t_valid - 1) % c

    @pl.when(ci == n_chunks - 1)
    def _():
        shift_ref[0] = z[last_row:last_row + 1, :]

    zm = z + mu_ref[...] * (z_prev - z)
    r = zm[:, 0:RWKV_WIDTH]
    k = zm[:, RWKV_WIDTH:2 * RWKV_WIDTH]
    v = zm[:, 2 * RWKV_WIDTH:3 * RWKV_WIDTH]
    l0 = zm[:, LORA_OFF:LORA_OFF + LANES]
    l1 = zm[:, LORA_OFF + LANES:RWKV_PROJ_PAD]

    w0 = vecs_ref[0:1, :]
    a0 = vecs_ref[1:2, :]
    k_k = vecs_ref[2:3, :]
    k_a = vecs_ref[3:4, :]
    r_k = vecs_ref[4:5, :]
    lnx_w = vecs_ref[5:6, :]
    lnx_b = vecs_ref[6:7, :]

    wl = w0 + _dot(jnp.tanh(l0).astype(BF16), ww_ref[...])
    neg = -wl
    softplus = jnp.maximum(neg, 0.0) + jnp.log(1.0 + jnp.exp(-jnp.abs(neg)))
    logw = -jnp.exp(-softplus - 0.5)
    a = jax.nn.sigmoid(a0 + _dot(l0.astype(BF16), wa_ref[...]))
    g = _dot(jax.nn.sigmoid(l1).astype(BF16), wg_ref[...])

    seg = seg_ref[...]
    segt = segt_ref[...]

    def head_sum(x):
        hi, lo = _split2(x)
        s = _dot(hi, seg) + _dot(lo, seg)
        shi, slo = _split2(s)
        return _dot(shi, segt) + _dot(slo, segt)

    kk = k * k_k
    kk = kk / jnp.maximum(jnp.sqrt(head_sum(kk * kk)), 1e-12)
    k_h = k * (1.0 + (a - 1.0) * k_a)
    if t_valid % c != 0:
        live = (ci * c + row) < t_valid
        logw = jnp.where(live, logw, 0.0)
        kk = jnp.where(live, kk, 0.0)
        k_h = jnp.where(live, k_h, 0.0)
    b = kk * a

    ri = lax.broadcasted_iota(jnp.int32, (c, c), 0)
    cj = lax.broadcasted_iota(jnp.int32, (c, c), 1)
    tri = jnp.where(ri >= cj, 1.0, 0.0).astype(BF16)
    h1, h2, h3 = _split3(logw)
    lc = _dot(tri, h1) + _dot(tri, h2) + _dot(tri, h3)
    lc_end = lc[c - 1:c, :]
    inv_g = jnp.exp(-lc)
    r_hat = r * jnp.exp(lc)
    kap_hat = kk * jnp.exp(lc - logw)
    k_hat = k_h * inv_g
    b_hat = b * inv_g
    to_end = jnp.exp(lc_end - lc)
    k_bar = k_h * to_end
    b_bar = b * to_end
    g_end = jnp.exp(lc_end)

    gw = RWKV_GROUP * hd
    cg = RWKV_GROUP * c
    head_of_lane = lax.broadcasted_iota(jnp.int32, (1, gw), 1) // hd

    def stack(x):
        return jnp.concatenate([jnp.where(head_of_lane == j, x, 0.0) for j in range(RWKV_GROUP)],
                               axis=0).astype(BF16)

    ig = lax.broadcasted_iota(jnp.int32, (cg, cg), 0)
    jg = lax.broadcasted_iota(jnp.int32, (cg, cg), 1)
    strict = ig > jg
    lower = ig >= jg
    eye = jnp.where(ig == jg, 1.0, 0.0)

    def level(s):
        return (((ig ^ jg) & (-2 * s)) == 0) & ((ig & s) != 0) & ((jg & s) == 0)

    fused = cg % LANES == 0
    groups = range(RWKV_HEADS // RWKV_GROUP)
    lanes = [slice(grp * gw, (grp + 1) * gw) for grp in groups]
    s_old = [state[grp] for grp in groups]
    s_bf = [s.astype(BF16) for s in s_old]
    lhs = [jnp.concatenate([stack(kap_hat[:, sl]), stack(r_hat[:, sl])], axis=0) for sl in lanes]
    kh_s = [stack(k_hat[:, sl]) for sl in lanes]
    bh_s = [stack(b_hat[:, sl]) for sl in lanes]
    v_s = [stack(v[:, sl]) for sl in lanes]
    nn, mm, ark, arb = [], [], [], []
    for grp in groups:
        if fused:
            sc = _dot_nt(lhs[grp], jnp.concatenate([bh_s[grp], kh_s[grp]], axis=0))
            nn_raw, mm_raw = sc[0:cg, 0:cg], sc[0:cg, cg:2 * cg]
            arb_raw, ark_raw = sc[cg:2 * cg, 0:cg], sc[cg:2 * cg, cg:2 * cg]
        else:
            kap_s, r_s = lhs[grp][0:cg], lhs[grp][cg:2 * cg]
            nn_raw, mm_raw = _dot_nt(kap_s, bh_s[grp]), _dot_nt(kap_s, kh_s[grp])
            arb_raw, ark_raw = _dot_nt(r_s, bh_s[grp]), _dot_nt(r_s, kh_s[grp])
        nn.append(jnp.where(strict, nn_raw, 0.0))
        mm.append(jnp.where(strict, mm_raw, 0.0))
        ark.append(jnp.where(lower, ark_raw, 0.0))
        arb.append(jnp.where(lower, arb_raw, 0.0))
    from_state = [_dot_nt(lhs[grp], s_bf[grp]) for grp in groups]
    from_v = [_dot(jnp.concatenate([mm[grp], ark[grp]], axis=0).astype(BF16), v_s[grp]) for grp in groups]
    rhs = [-(from_state[grp][0:cg] + from_v[grp][0:cg]) for grp in groups]
    t_inv = [eye - jnp.where(level(1), nn[grp], 0.0) for grp in groups]
    s = 2
    while s < c:
        t_bf = [t.astype(BF16) for t in t_inv]
        half = [_dot(t_bf[grp], jnp.where(level(s), nn[grp], 0.0).astype(BF16)).astype(BF16) for grp in groups]
        t_inv = [t_inv[grp] - _dot(half[grp], t_bf[grp]) for grp in groups]
        s *= 2
    e_s = [_dot(t_inv[grp].astype(BF16), rhs[grp].astype(BF16)).astype(BF16) for grp in groups]
    y_s = [from_state[grp][cg:2 * cg] + from_v[grp][cg:2 * cg] + _dot(arb[grp].astype(BF16), e_s[grp])
           for grp in groups]
    ys = []
    for grp in groups:
        y_g = y_s[grp][0:c]
        for j in range(1, RWKV_GROUP):
            y_g = y_g + y_s[grp][j * c:(j + 1) * c]
        ys.append(y_g)
    y = jnp.concatenate(ys, axis=1)
    s_new = [s_old[grp] * g_end[:, lanes[grp]]
             + _dot_tn(jnp.concatenate([v_s[grp], e_s[grp]], axis=0),
                       jnp.concatenate([stack(k_bar[:, lanes[grp]]), stack(b_bar[:, lanes[grp]])], axis=0))
             for grp in groups]
    for grp in groups:
        state[grp] = s_new[grp]

    inv_n = 1.0 / RWKV_HEAD_DIM
    mean = head_sum(y) * inv_n
    yc = y - mean
    var = head_sum(yc * yc) * inv_n
    yn = yc * lax.rsqrt(var + GN_EPS) * lnx_w + lnx_b
    bonus = head_sum(r * k_h * r_k) * v
    y_ref[0] = ((yn + bonus) * g).astype(y_ref.dtype)

    @pl.when(ci == n_chunks - 1)
    def _():
        for h in range(RWKV_HEADS):
            grp, sl = head_block(h)
            sfin_ref[0, h] = state[grp, sl, sl]


def _rwkv_branch(z, shift0, s0, lw, *, chunk, t_valid, s0_layer=None):
    g_, t_, pr = z.shape
    nc = t_ // chunk
    kern = functools.partial(_rwkv_kernel, chunk=chunk, t_valid=t_valid)
    const2 = lambda gi, ci: (0, 0)
    state_block = (1, RWKV_HEADS, RWKV_HEAD_DIM, RWKV_HEAD_DIM)
    group_lanes = RWKV_GROUP * RWKV_HEAD_DIM
    if s0_layer is None:
        s0_spec = pl.BlockSpec(state_block, lambda gi, ci: (gi, 0, 0, 0))
    else:
        s0_spec = pl.BlockSpec((None,) + state_block, lambda gi, ci: (s0_layer, gi, 0, 0, 0))
    return pl.pallas_call(
        kern,
        grid=(g_, nc),
        in_specs=[
            pl.BlockSpec((1, chunk, pr), lambda gi, ci: (gi, ci, 0)),
            pl.BlockSpec((1, 1, pr), lambda gi, ci: (gi, 0, 0)),
            s0_spec,
            pl.BlockSpec((1, pr), const2),
            pl.BlockSpec((SUBLANES, RWKV_WIDTH), const2),
            pl.BlockSpec((LANES, RWKV_WIDTH), const2),
            pl.BlockSpec((LANES, RWKV_WIDTH), const2),
            pl.BlockSpec((2 * LANES, RWKV_WIDTH), const2),
            pl.BlockSpec((RWKV_WIDTH, LANES), const2),
            pl.BlockSpec((LANES, RWKV_WIDTH), const2),
        ],
        out_specs=[
            pl.BlockSpec((1, chunk, RWKV_WIDTH), lambda gi, ci: (gi, ci, 0)),
            pl.BlockSpec(state_block, lambda gi, ci: (gi, 0, 0, 0)),
            pl.BlockSpec((1, 1, pr), lambda gi, ci: (gi, 0, 0)),
        ],
        out_shape=[
            jax.ShapeDtypeStruct((g_, t_, RWKV_WIDTH), BF16),
            jax.ShapeDtypeStruct((g_,) + state_block[1:], F32),
            jax.ShapeDtypeStruct((g_, 1, pr), F32),
        ],
        scratch_shapes=[pltpu.VMEM((1, pr), F32),
                        pltpu.VMEM((RWKV_HEADS // RWKV_GROUP, group_lanes, group_lanes), F32)],
        compiler_params=_cparams("parallel", "arbitrary"),
        name="rwkv7_chunk",
    )(z, shift0, s0, lw["mu"], lw["vecs"], lw["ww"], lw["wa"], lw["wg"], lw["seg"], lw["segt"])


def _rwkv_weights(l, mu_shift, w0, w_lora, a0, a_lora, g_lora, k_k, k_a, r_k, lnx_w, lnx_b):
    pad = RWKV_PROJ_PAD - RWKV_PROJ
    vecs = jnp.stack([w0[l], a0[l], k_k[l], k_a[l], r_k[l].reshape(-1), lnx_w[l], lnx_b[l],
                      jnp.zeros((RWKV_WIDTH,), F32)])
    zeros64 = jnp.zeros((DECAY_LORA, RWKV_WIDTH), F32)
    head_of_lane = np.arange(RWKV_WIDTH) // RWKV_HEAD_DIM
    seg = (head_of_lane[:, None] == np.arange(LANES)[None, :]).astype(np.float32)
    return {
        "mu": jnp.pad(mu_shift[l], (0, pad))[None, :],
        "vecs": vecs,
        "ww": jnp.concatenate([w_lora[l], zeros64]).astype(BF16),
        "wa": jnp.concatenate([zeros64, a_lora[l]]).astype(BF16),
        "wg": jnp.pad(g_lora[l], ((0, 2 * LANES - GATE_LORA), (0, 0))).astype(BF16),
        "seg": jnp.asarray(seg, BF16),
        "segt": jnp.asarray(seg.T, BF16),
    }


def _ada_kernel(c_ref, w_ref, b_ref, o_ref):
    c = c_ref[...]
    act = (c * jax.nn.sigmoid(c)).astype(BF16)
    o_ref[0] = _dot(act, w_ref[0].astype(BF16)) + b_ref[0]


def _ada_mod(c_all, w_ada, b_ada, *, tn=1024):
    r_, d = c_all.shape
    l_, _, n = w_ada.shape
    return pl.pallas_call(
        _ada_kernel,
        grid=(l_, n // tn),
        in_specs=[
            pl.BlockSpec((r_, d), lambda li, j: (0, 0)),
            pl.BlockSpec((1, d, tn), lambda li, j: (li, 0, j)),
            pl.BlockSpec((1, 1, tn), lambda li, j: (li, 0, j)),
        ],
        out_specs=pl.BlockSpec((1, r_, tn), lambda li, j: (li, 0, j)),
        out_shape=jax.ShapeDtypeStruct((l_, r_, n), F32),
        compiler_params=_cparams("parallel", "parallel"),
        name="ada_mod",
    )(c_all, w_ada, b_ada.reshape(l_, 1, n))


def _mod_spec(mod, chunk, width, tt):
    per_token = mod.shape[1] != 1
    rows = tt if per_token else 1
    blocks_per_chunk = D_MODEL // width

    def index(gi, ti, *rest):
        j = rest[0] if rest else 0
        return (gi, ti if per_token else 0, chunk * blocks_per_chunk + j)

    return pl.BlockSpec((1, rows, width), index)


def _norm_kernel(x_ref, gain_ref, sc_ref, sh_ref, o_ref):
    x = x_ref[0]
    y = x * lax.rsqrt(jnp.mean(x * x, axis=-1, keepdims=True) + RMS_EPS) * gain_ref[...]
    o_ref[0] = (y * (1.0 + sc_ref[0]) + sh_ref[0]).astype(o_ref.dtype)


def _norm_mod(x, gain, mod, sh_chunk, sc_chunk, *, tt):
    g_, t_, d = x.shape
    return pl.pallas_call(
        _norm_kernel,
        grid=(g_, t_ // tt),
        in_specs=[
            pl.BlockSpec((1, tt, d), lambda gi, ti: (gi, ti, 0)),
            pl.BlockSpec((1, d), lambda gi, ti: (0, 0)),
            _mod_spec(mod, sc_chunk, d, tt),
            _mod_spec(mod, sh_chunk, d, tt),
        ],
        out_specs=pl.BlockSpec((1, tt, d), lambda gi, ti: (gi, ti, 0)),
        out_shape=jax.ShapeDtypeStruct((g_, t_, d), BF16),
        compiler_params=_cparams("parallel", "parallel"),
        name="norm_mod",
    )(x, gain.reshape(1, d), mod, mod)


def _matmul(name, lhs, ws, pairs, epilogue, out_dtypes, *, tm, tn, tiles=(), mods=(), cols=()):
    g_, t_, _ = lhs[0].shape
    w_arrays = [w[0] if isinstance(w, tuple) else w for w in ws]
    n = w_arrays[0].shape[-1]
    nl, nw, nt, nm, nc = len(lhs), len(ws), len(tiles), len(mods), len(cols)

    def w_spec(w):
        if isinstance(w, tuple):
            layer = w[1]
            return pl.BlockSpec((None, w[0].shape[1], tn), lambda gi, ti, j: (layer, 0, j))
        return pl.BlockSpec((w.shape[0], tn), lambda gi, ti, j: (0, j))

    def body(*refs):
        lhs_refs = refs[:nl]
        w_refs = refs[nl:nl + nw]
        tile_refs = refs[nl + nw:nl + nw + nt]
        mod_refs = refs[nl + nw + nt:nl + nw + nt + nm]
        col_refs = refs[nl + nw + nt + nm:nl + nw + nt + nm + nc]
        out_refs = refs[nl + nw + nt + nm + nc:]
        dots = [_dot(lhs_refs[pi][0], w_refs[i][...].astype(BF16)) for i, pi in enumerate(pairs)]
        outs = epilogue(dots, [r[0] for r in tile_refs], [r[0] for r in mod_refs], [r[...] for r in col_refs])
        for o_ref, o in zip(out_refs, outs):
            o_ref[0] = o.astype(o_ref.dtype)

    in_specs = [pl.BlockSpec((1, tm, a.shape[2]), lambda gi, ti, j: (gi, ti, 0)) for a in lhs]
    in_specs += [w_spec(w) for w in ws]
    in_specs += [pl.BlockSpec((1, tm, tn), lambda gi, ti, j: (gi, ti, j)) for _ in tiles]
    in_specs += [_mod_spec(m, chunk, tn, tm) for m, chunk in mods]
    in_specs += [pl.BlockSpec((1, tn), lambda gi, ti, j: (0, j)) for _ in cols]
    out = pl.pallas_call(
        body,
        grid=(g_, t_ // tm, n // tn),
        in_specs=in_specs,
        out_specs=[pl.BlockSpec((1, tm, tn), lambda gi, ti, j: (gi, ti, j)) for _ in out_dtypes],
        out_shape=[jax.ShapeDtypeStruct((g_, t_, n), dt) for dt in out_dtypes],
        compiler_params=_cparams("parallel", "parallel", "arbitrary"),
        name=name,
    )(*lhs, *w_arrays, *tiles, *[m for m, _ in mods], *cols)
    return out


def _ep_plain(dots, tiles, mods, cols):
    return [dots[0]]


def _ep_plain2(dots, tiles, mods, cols):
    return [dots[0], dots[0]]


def _ep_head_norm(dots, tiles, mods, cols):
    x = dots[0]
    parts = []
    for j in range(x.shape[1] // MOBA_HEAD_DIM):
        xs = x[:, j * MOBA_HEAD_DIM:(j + 1) * MOBA_HEAD_DIM]
        parts.append(xs * lax.rsqrt(jnp.mean(xs * xs, axis=-1, keepdims=True) + RMS_EPS))
    y = jnp.concatenate(parts, axis=1) * cols[0]
    return [y, y]


def _ep_merge(dots, tiles, mods, cols):
    return [jax.nn.sigmoid(dots[0]) * dots[2] + jax.nn.sigmoid(dots[1]) * dots[3]]


def _ep_residual(dots, tiles, mods, cols):
    return [tiles[0] + mods[0] * dots[0]]


def _ep_swiglu(dots, tiles, mods, cols):
    gate = dots[0]
    return [gate * jax.nn.sigmoid(gate) * dots[1]]


def _t5_bucket(dist):
    n = jnp.maximum(dist, 0)
    max_exact = NUM_BUCKETS // 2
    nf = jnp.maximum(n, 1).astype(F32)
    large = max_exact + (jnp.log(nf / max_exact) / math.log(MAX_DISTANCE / max_exact)
                         * (NUM_BUCKETS - max_exact)).astype(jnp.int32)
    large = jnp.minimum(large, NUM_BUCKETS - 1)
    return jnp.where(n < max_exact, n, large)


def _top_blocks(gates, lane_f):
    sel = jnp.zeros_like(gates)
    for _ in range(MOBA_TOPK):
        mx = jnp.max(gates, axis=-1, keepdims=True)
        first = jnp.min(jnp.where(gates == mx, lane_f, float(LANES)), axis=-1, keepdims=True)
        pick = (lane_f == first) & (mx > 0.5 * NEG_BIG)
        sel = jnp.where(pick, 1.0, sel)
        gates = jnp.where(pick, NEG_BIG, gates)
    return sel


def _kmean_kernel(k_ref, o_ref):
    n = pl.program_id(1)
    o_ref[0, pl.ds(n, 1), :] = jnp.mean(k_ref[0], axis=0, keepdims=True)


def _block_means(qk):
    b_, t_, _ = qk.shape
    nb = t_ // MOBA_BLOCK
    return pl.pallas_call(
        _kmean_kernel,
        grid=(b_, nb),
        in_specs=[pl.BlockSpec((1, MOBA_BLOCK, MOBA_WIDTH), lambda bi, n: (bi, n, 1))],
        out_specs=pl.BlockSpec((1, nb, MOBA_WIDTH), lambda bi, n: (bi, 0, 0)),
        out_shape=jax.ShapeDtypeStruct((b_, nb, MOBA_WIDTH), F32),
        compiler_params=_cparams("parallel", "arbitrary"),
        name="moba_block_means",
    )(qk)


def _bias_tile_kernel(rt_ref, o_ref, *, n_blocks):
    delta = pl.program_id(1)
    blk = MOBA_BLOCK
    start = pl.multiple_of((n_blocks - 1 - delta) * blk, blk)
    window = rt_ref[0, :, pl.ds(start, 2 * blk)]
    rolled = pltpu.roll(jnp.broadcast_to(window, (blk, 2 * blk)), 0, 1, stride=1, stride_axis=0)
    r_i = lax.broadcasted_iota(jnp.int32, (blk, blk), 0)
    c_i = lax.broadcasted_iota(jnp.int32, (blk, blk), 1)
    on_diag = (delta == 0).astype(jnp.int32)
    o_ref[0, 0] = jnp.where((c_i - r_i) * on_diag <= 0, rolled[:, blk:2 * blk], NEG_BIG)


def _bias_tiles(rev_bias, n_blocks):
    h_ = rev_bias.shape[0]
    blk = MOBA_BLOCK
    return pl.pallas_call(
        functools.partial(_bias_tile_kernel, n_blocks=n_blocks),
        grid=(h_, n_blocks),
        in_specs=[pl.BlockSpec((1, 1, rev_bias.shape[2]), lambda h, dl: (h, 0, 0))],
        out_specs=pl.BlockSpec((1, 1, blk, blk), lambda h, dl: (h, dl, 0, 0)),
        out_shape=jax.ShapeDtypeStruct((h_, n_blocks, blk, blk), F32),
        compiler_params=_cparams("parallel", "parallel"),
        name="moba_bias_tiles",
    )(rev_bias)


def _moba_prompt_kernel(q_ref, k_ref, v_ref, km_ref, bias_ref, o_ref, qa_sc, m_sc, l_sc, acc_sc, *, n_blocks):
    i = pl.program_id(2)
    blk = MOBA_BLOCK
    nb = n_blocks
    q = q_ref[0]
    km1, km2, km3 = _split3(km_ref[0])
    q1, q2, q3 = _split3(q)
    g1 = _dot_nt(jnp.concatenate([km1, km2, km3], axis=0), q1)
    g2 = _dot_nt(jnp.concatenate([km1, km2], axis=0), q2)
    g3 = _dot_nt(km1, q3)
    gates = (g1[2 * nb:3 * nb] + g2[nb:2 * nb] + g3) + (g1[nb:2 * nb] + g2[0:nb]) + g1[0:nb]
    n_i = lax.broadcasted_iota(jnp.int32, (nb, blk), 0)
    n_f = n_i.astype(F32)
    gates = jnp.where(n_i < i, gates, NEG_BIG)
    sel = jnp.where(n_i == i, 1.0, 0.0)
    for _ in range(MOBA_TOPK):
        mx = jnp.max(gates, axis=0, keepdims=True)
        first = jnp.min(jnp.where(gates == mx, n_f, float(nb)), axis=0, keepdims=True)
        pick = (n_f == first) & (mx > 0.5 * NEG_BIG)
        sel = jnp.where(pick, 1.0, sel)
        gates = jnp.where(pick, NEG_BIG, gates)
    hidden = jnp.where(sel > 0.0, 0.0, NEG_BIG)
    hidden = jnp.concatenate([hidden, jnp.zeros((LANES - nb, blk), F32)], axis=0).T
    qa_sc[...] = jnp.concatenate([(q * (MOBA_HEAD_DIM ** -0.5)).astype(BF16), hidden.astype(BF16)], axis=1)
    m_sc[...] = jnp.full(m_sc.shape, NEG_BIG, F32)
    l_sc[...] = jnp.zeros(l_sc.shape, F32)
    acc_sc[...] = jnp.zeros(acc_sc.shape, F32)
    lane_i = lax.broadcasted_iota(jnp.int32, (blk, LANES), 1)

    def logits_of_pair(jj):
        qa = qa_sc[...]
        parts = []
        for j_raw in (2 * jj, 2 * jj + 1):
            j = jnp.minimum(j_raw, nb - 1)
            rows = pl.ds(pl.multiple_of(j * blk, blk), blk)
            k_aug = jnp.concatenate([k_ref[0, rows, :], jnp.where(lane_i == j_raw, 1.0, 0.0).astype(BF16)], axis=1)
            parts.append(_dot_nt(qa, k_aug) + bias_ref[0, jnp.clip(i - j_raw, 0, nb - 1)])
        return jnp.concatenate(parts, axis=1)

    def pair(jj, s_cur):
        s_next = logits_of_pair(jj + 1)
        vs = [v_ref[0, pl.ds(pl.multiple_of(j * blk, blk), blk), :] for j in (2 * jj, 2 * jj + 1)]
        cols = [s_cur[:, x * LANES:(x + 1) * LANES] for x in range(2 * blk // LANES)]
        top = cols[0]
        for x in cols[1:]:
            top = jnp.maximum(top, x)
        m_prev = m_sc[...]
        m_new = jnp.maximum(m_prev, jnp.max(top, axis=-1, keepdims=True))
        ps = [jnp.exp(x - m_new) for x in cols]
        tot = ps[0]
        for x in ps[1:]:
            tot = tot + x
        alpha = jnp.exp(m_prev - m_new)
        l_sc[...] = alpha * l_sc[...] + jnp.sum(tot, axis=-1, keepdims=True)
        p_all = jnp.concatenate(ps, axis=1).astype(BF16)
        acc_sc[...] = alpha * acc_sc[...] + _dot(p_all, jnp.concatenate(vs, axis=0))
        m_sc[...] = m_new
        return s_next

    lax.fori_loop(0, (i + 2) // 2, pair, logits_of_pair(0))
    o_ref[0] = (acc_sc[...] / l_sc[...]).astype(o_ref.dtype)


def _moba_prompt(qk, qk_bf, v_bf, kmean, bias_tiles):
    b_, t_, _ = qk.shape
    nb = t_ // MOBA_BLOCK
    h_ = MOBA_HEADS
    blk = MOBA_BLOCK
    return pl.pallas_call(
        functools.partial(_moba_prompt_kernel, n_blocks=nb),
        grid=(b_, h_, nb),
        in_specs=[
            pl.BlockSpec((1, blk, LANES), lambda bi, h, i: (bi, i, h)),
            pl.BlockSpec((1, t_, LANES), lambda bi, h, i: (bi, 0, h_ + h)),
            pl.BlockSpec((1, t_, LANES), lambda bi, h, i: (bi, 0, h)),
            pl.BlockSpec((1, nb, LANES), lambda bi, h, i: (bi, 0, h)),
            pl.BlockSpec((1, nb, blk, blk), lambda bi, h, i: (h, 0, 0, 0)),
        ],
        out_specs=pl.BlockSpec((1, blk, LANES), lambda bi, h, i: (bi, i, h)),
        out_shape=jax.ShapeDtypeStruct((b_, t_, MOBA_WIDTH), BF16),
        scratch_shapes=[pltpu.VMEM((blk, 2 * LANES), BF16), pltpu.VMEM((blk, LANES), F32),
                        pltpu.VMEM((blk, LANES), F32), pltpu.VMEM((blk, LANES), F32)],
        compiler_params=_cparams("parallel", "parallel", "arbitrary"),
        name="moba_prompt",
    )(qk, qk_bf, v_bf, kmean, bias_tiles)


def _prompt_bias_table(rel_bias, t_):
    y = jnp.arange(t_ + MOBA_BLOCK, dtype=jnp.int32)
    tab = rel_bias[_t5_bucket(t_ - y)]
    tab = jnp.where((y <= t_)[:, None], tab, 0.0)
    return tab.T.reshape(MOBA_HEADS, 1, t_ + MOBA_BLOCK)


def _moba_sample_kernel(pt_ref, q_ref, *refs, n_past):
    n_pages = SAMPLE_BLOCKS_PER_STEP * PAGES_PER_BLOCK
    k_refs, v_refs = refs[:n_pages], refs[n_pages:2 * n_pages]
    kn_ref, vn_ref, bias_ref, ownb_ref, o_ref, m_sc, l_sc, g_sc, acc_sc = refs[2 * n_pages:]
    step = pl.program_id(1)
    scale = MOBA_HEAD_DIM ** -0.5
    rows_n = q_ref.shape[1]
    cols_n = MOBA_BLOCK * MOBA_HEADS
    q = q_ref[0]
    q_bf = (q * scale).astype(BF16)
    shape = (rows_n, LANES)
    for blk in range(SAMPLE_BLOCKS_PER_STEP):
        n = step * SAMPLE_BLOCKS_PER_STEP + blk
        pages = slice(blk * PAGES_PER_BLOCK, (blk + 1) * PAGES_PER_BLOCK)
        kf = jnp.concatenate([r_[...] for r_ in k_refs[pages]], axis=0)
        vf = jnp.concatenate([r_[...] for r_ in v_refs[pages]], axis=0)
        logits = _dot_nt(q_bf, kf.astype(BF16)) + bias_ref[:, pl.ds(pl.multiple_of(n * cols_n, cols_n), cols_n)]
        m_b = jnp.max(logits, axis=-1, keepdims=True)
        p = jnp.exp(logits - m_b)
        m_sc[n] = jnp.broadcast_to(m_b, shape)
        l_sc[n] = jnp.broadcast_to(jnp.sum(p, axis=-1, keepdims=True), shape)
        acc_sc[n] = _dot(p.astype(BF16), vf.astype(BF16))
        kmean = jnp.mean(kf.reshape(MOBA_BLOCK, MOBA_HEADS, MOBA_HEAD_DIM), axis=0)
        gate = jnp.concatenate([jnp.sum(q[t * MOBA_HEADS:(t + 1) * MOBA_HEADS] * kmean, axis=-1, keepdims=True)
                                for t in range(rows_n // MOBA_HEADS)], axis=0)
        g_sc[n] = jnp.broadcast_to(gate, shape)

    @pl.when(step == pl.num_programs(1) - 1)
    def _():
        lane_f = lax.broadcasted_iota(jnp.int32, (1, LANES), 1).astype(F32)
        gates = jnp.full(shape, NEG_BIG, F32)
        for nb in range(n_past):
            gates = jnp.where(lane_f == float(nb), g_sc[nb], gates)
        sel = _top_blocks(gates, lane_f)
        s_own = _dot_nt(q_bf, kn_ref[0].astype(BF16)) + ownb_ref[...]
        m_tot = jnp.broadcast_to(jnp.max(s_own, axis=-1, keepdims=True), shape)
        picked = []
        for nb in range(n_past):
            on = jnp.max(jnp.where(lane_f == float(nb), sel, 0.0), axis=-1, keepdims=True) > 0.0
            picked.append(on)
            m_tot = jnp.maximum(m_tot, jnp.where(on, m_sc[nb], NEG_BIG))
        p_own = jnp.exp(s_own - m_tot[:, 0:rows_n])
        l_tot = jnp.broadcast_to(jnp.sum(p_own, axis=-1, keepdims=True), shape)
        acc = _dot(p_own.astype(BF16), vn_ref[0].astype(BF16))
        for nb in range(n_past):
            w = jnp.where(picked[nb], jnp.exp(m_sc[nb] - m_tot), 0.0)
            l_tot = l_tot + w * l_sc[nb]
            acc = acc + w * acc_sc[nb]
        o_ref[0] = (acc / l_tot).astype(o_ref.dtype)


def _moba_sample(layer, page_table, q, k_new, v_new, cache_k, cache_v, bias_tab, own_bias):
    s_, rows_n, _ = q.shape
    n_past = page_table.shape[1] * PAGE_SIZE // MOBA_BLOCK
    row = pl.BlockSpec((1, rows_n, MOBA_HEAD_DIM), lambda si, n, pt: (si, 0, 0))

    pages_per_step = SAMPLE_BLOCKS_PER_STEP * PAGES_PER_BLOCK

    def page_spec(pg):
        return pl.BlockSpec((None, None, PAGE_SIZE * MOBA_HEADS, MOBA_HEAD_DIM),
                            lambda si, n, pt: (layer, pt[si, pages_per_step * n + pg], 0, 0))

    tiles = [page_spec(pg) for pg in range(pages_per_step)]
    rows_shape = cache_k.shape[:2] + (PAGE_SIZE * MOBA_HEADS, MOBA_HEAD_DIM)
    cache_k = cache_k.reshape(rows_shape)
    cache_v = cache_v.reshape(rows_shape)
    grid_spec = pltpu.PrefetchScalarGridSpec(
        num_scalar_prefetch=1,
        grid=(s_, n_past // SAMPLE_BLOCKS_PER_STEP),
        in_specs=[row] + tiles + tiles + [
            row,
            row,
            pl.BlockSpec(bias_tab.shape, lambda si, n, pt: (0, 0)),
            pl.BlockSpec(own_bias.shape, lambda si, n, pt: (0, 0)),
        ],
        out_specs=row,
        scratch_shapes=[pltpu.VMEM((n_past, rows_n, LANES), F32) for _ in range(4)],
    )
    return pl.pallas_call(
        functools.partial(_moba_sample_kernel, n_past=n_past),
        grid_spec=grid_spec,
        out_shape=jax.ShapeDtypeStruct((s_, rows_n, MOBA_HEAD_DIM), BF16),
        compiler_params=_cparams("parallel", "arbitrary"),
        name="moba_sample",
    )(page_table, q, *([cache_k] * len(tiles)), *([cache_v] * len(tiles)), k_new, v_new, bias_tab, own_bias)


def _sample_bias_tables(rel_bias, past_len, t_new):
    h_ = rel_bias.shape[1]
    same_head = jnp.eye(h_, dtype=bool)
    dist = past_len + t_new - 1 - jnp.arange(past_len + t_new, dtype=jnp.int32)
    by_dist_rev = rel_bias[_t5_bucket(dist)]
    tab = jnp.stack([by_dist_rev[t_new - 1 - t:t_new - 1 - t + past_len] for t in range(t_new)])
    per_key = jnp.transpose(tab, (0, 2, 1)).reshape(t_new * h_, past_len)
    t = jnp.arange(t_new, dtype=jnp.int32)
    own = rel_bias[_t5_bucket(t[:, None] - t[None, :])]
    own = jnp.where((t[None, :] <= t[:, None])[..., None], own, NEG_BIG)
    own = jnp.where(same_head[None, :, None, :], jnp.transpose(own, (0, 2, 1))[..., None], NEG_BIG)
    return _spread_over_heads(per_key, h_), own.reshape(t_new * h_, t_new * h_)


def _spread_kernel(x_ref, e_ref, neg_ref, o_ref):
    hi, mid, lo = _split3(x_ref[...])
    e = e_ref[...]
    o_ref[...] = (_dot(hi, e) + _dot(mid, e) + _dot(lo, e)) + neg_ref[...]


def _spread_over_heads(per_key, h_):
    rows_n, n_keys = per_key.shape
    blk = MOBA_BLOCK
    spread = np.zeros((blk, blk * h_), np.float32)
    spread[np.arange(blk * h_) // h_, np.arange(blk * h_)] = 1.0
    other_head = np.where((np.arange(rows_n) % h_)[:, None] == (np.arange(blk * h_) % h_)[None, :], 0.0, NEG_BIG)
    return pl.pallas_call(
        _spread_kernel,
        grid=(n_keys // blk,),
        in_specs=[
            pl.BlockSpec((rows_n, blk), lambda n: (0, n)),
            pl.BlockSpec(spread.shape, lambda n: (0, 0)),
            pl.BlockSpec(other_head.shape, lambda n: (0, 0)),
        ],
        out_specs=pl.BlockSpec((rows_n, blk * h_), lambda n: (0, n)),
        out_shape=jax.ShapeDtypeStruct((rows_n, n_keys * h_), F32),
        compiler_params=_cparams("parallel"),
        name="moba_sample_bias",
    )(per_key, jnp.asarray(spread, BF16), jnp.asarray(other_head, F32))


PROMPT_ROW_TILE = 1024

SH1, SC1, G1, SH2, SC2, G2 = range(6)


def _layer_weights(l, w_in, q_norm, k_norm, w_up_a, w_up_b, w_out, w_ffn_gate, w_ffn_up, w_ffn_down):
    o_q = RWKV_PROJ
    o_v = o_q + 2 * MOBA_WIDTH
    o_ga = o_v + MOBA_WIDTH
    o_gb = o_ga + D_MODEL
    wl = w_in[l]
    heads = MOBA_WIDTH // MOBA_HEAD_DIM
    return {
        "rwkv": jnp.pad(wl[:, :RWKV_PROJ], ((0, 0), (0, RWKV_PROJ_PAD - RWKV_PROJ))).astype(BF16),
        "qk": wl[:, o_q:o_v].astype(BF16),
        "v": wl[:, o_v:o_ga].astype(BF16),
        "ga": wl[:, o_ga:o_gb].astype(BF16),
        "gb": wl[:, o_gb:].astype(BF16),
        "qk_gain": jnp.concatenate([jnp.tile(q_norm[l], heads), jnp.tile(k_norm[l], heads)])[None, :],
        "up_a": (w_up_a, l),
        "up_b": (w_up_b, l),
        "out": (w_out, l),
        "fg": (w_ffn_gate, l),
        "fu": (w_ffn_up, l),
        "fd": (w_ffn_down, l),
    }


def _trunk_layer(x, mod, norm_mix, norm_ffn, w, rwkv_fn, attn_fn, *, tm):
    h = _norm_mod(x, norm_mix, mod, SH1, SC1, tt=min(tm, 512))
    z_r = _matmul("in_rwkv", [h], [w["rwkv"]], [0], _ep_plain, [F32], tm=tm, tn=RWKV_PROJ_PAD // 3)[0]
    qk, qk_bf = _matmul("in_qk", [h], [w["qk"]], [0], _ep_head_norm, [F32, BF16], tm=tm, tn=1024,
                        cols=[w["qk_gain"]])
    v, v_bf = _matmul("in_v", [h], [w["v"]], [0], _ep_plain2, [F32, BF16], tm=tm, tn=1024)
    y_a, s_new, shift_new = rwkv_fn(z_r)
    y_b = attn_fn(qk, qk_bf, v, v_bf)
    merged = _matmul("merge", [h, y_a, y_b], [w["ga"], w["gb"], w["up_a"], w["up_b"]], [0, 0, 1, 2],
                     _ep_merge, [BF16], tm=tm, tn=512)[0]
    x1 = _matmul("out_proj", [merged], [w["out"]], [0], _ep_residual, [F32], tm=tm, tn=512,
                 tiles=[x], mods=[(mod, G1)])[0]
    h2 = _norm_mod(x1, norm_ffn, mod, SH2, SC2, tt=min(tm, 512))
    act = _matmul("ffn_in", [h2], [w["fg"], w["fu"]], [0, 0], _ep_swiglu, [BF16], tm=tm, tn=512)[0]
    x2 = _matmul("ffn_out", [act], [w["fd"]], [0], _ep_residual, [F32], tm=tm, tn=256,
                 tiles=[x1], mods=[(mod, G2)])[0]
    return x2, qk, v, s_new, shift_new


def kernel(x_prompt, x_sample, cache_k, cache_v, state_wkv, state_shift, page_table, c_prompt, c_sample, rel_bias, w_ada, b_ada, norm_mix, norm_ffn, w_in, mu_shift, w0, w_lora, a0, a_lora, g_lora, k_k, k_a, r_k, lnx_w, lnx_b, q_norm, k_norm, w_up_a, w_up_b, w_out, w_ffn_gate, w_ffn_up, w_ffn_down):
    bp, tp, d = x_prompt.shape
    bs, ts, _ = x_sample.shape
    n_seq = bp + bs
    c_all = jnp.concatenate([c_prompt, c_sample])
    c_all = jnp.pad(c_all, ((0, -n_seq % SUBLANES), (0, 0)))
    mod = _ada_mod(c_all, w_ada, b_ada)

    past_len = page_table.shape[1] * PAGE_SIZE
    bias_tiles = _bias_tiles(_prompt_bias_table(rel_bias, tp), tp // MOBA_BLOCK)
    bias_tab, own_bias = _sample_bias_tables(rel_bias, past_len, ts)
    pr_pad = RWKV_PROJ_PAD - RWKV_PROJ
    sample_chunk = SUBLANES

    def sample_rows(a):
        return a.reshape(bs, ts * MOBA_HEADS, MOBA_HEAD_DIM)

    x_p = x_prompt
    x_s = x_sample.reshape(1, bs * ts, d)
    outs = [[] for _ in range(8)]
    for l in range(DEPTH):
        w = _layer_weights(l, w_in, q_norm, k_norm, w_up_a, w_up_b, w_out, w_ffn_gate, w_ffn_up, w_ffn_down)
        lw = _rwkv_weights(l, mu_shift, w0, w_lora, a0, a_lora, g_lora, k_k, k_a, r_k, lnx_w, lnx_b)
        mod_p = mod[l, :bp][:, None, :]
        mod_s = jnp.repeat(mod[l, bp:n_seq], ts, axis=0)[None]

        def rwkv_p(z_r):
            shift0 = jnp.zeros((bp, 1, RWKV_PROJ_PAD), F32)
            s0 = jnp.zeros((bp, RWKV_HEADS, RWKV_HEAD_DIM, RWKV_HEAD_DIM), F32)
            return _rwkv_branch(z_r, shift0, s0, lw, chunk=64, t_valid=tp)

        def attn_p(qk, qk_bf, v, v_bf):
            return _moba_prompt(qk, qk_bf, v_bf, _block_means(qk), bias_tiles)

        x_p, qk_p, v_p, s_p, sh_p = _trunk_layer(x_p, mod_p, norm_mix[l], norm_ffn[l], w, rwkv_p, attn_p,
                                                 tm=min(tp, PROMPT_ROW_TILE))

        def rwkv_s(z_r):
            z = z_r.reshape(bs, ts, RWKV_PROJ_PAD)
            z = jnp.pad(z, ((0, 0), (0, sample_chunk - ts), (0, 0)))
            shift0 = jnp.pad(state_shift[l], ((0, 0), (0, pr_pad)))[:, None, :]
            y, s_new, shift_new = _rwkv_branch(z, shift0, state_wkv, lw, chunk=sample_chunk, t_valid=ts,
                                               s0_layer=l)
            return y[:, :ts].reshape(1, bs * ts, RWKV_WIDTH), s_new, shift_new

        def attn_s(qk, qk_bf, v, v_bf):
            y = _moba_sample(l, page_table, sample_rows(qk[..., :MOBA_WIDTH]), sample_rows(qk[..., MOBA_WIDTH:]),
                             sample_rows(v), cache_k, cache_v, bias_tab, own_bias)
            return y.reshape(1, bs * ts, MOBA_WIDTH)

        x_s, qk_s, v_s, s_s, sh_s = _trunk_layer(x_s, mod_s, norm_mix[l], norm_ffn[l], w, rwkv_s, attn_s,
                                                 tm=bs * ts)

        kv_shape_p = (bp, tp, MOBA_HEADS, MOBA_HEAD_DIM)
        kv_shape_s = (bs, ts, MOBA_HEADS, MOBA_HEAD_DIM)
        for lst, val in zip(outs, [
                qk_p[..., MOBA_WIDTH:].reshape(kv_shape_p), v_p.reshape(kv_shape_p),
                s_p, sh_p[:, 0, :RWKV_PROJ],
                qk_s[..., MOBA_WIDTH:].reshape(kv_shape_s), v_s.reshape(kv_shape_s),
                s_s, sh_s[:, 0, :RWKV_PROJ]]):
            lst.append(val)
    return (x_p, x_s.reshape(bs, ts, d)) + tuple(jnp.stack(o) for o in outs)
```

```python
import functools
import math

import jax
import jax.numpy as jnp
import numpy as np
from jax import lax
from jax.experimental import pallas as pl
from jax.experimental.pallas import tpu as pltpu

F32 = jnp.float32
BF16 = jnp.bfloat16

LANES = 128
SUBLANES = 8
VMEM_LIMIT_BYTES = 56 * 1024 * 1024

D_MODEL = 2048
DEPTH = 2
PAGE_SIZE = 128
RWKV_WIDTH = 1024
RWKV_HEAD_DIM = 64
RWKV_HEADS = 16
RWKV_GROUP = 4
DECAY_LORA = 64
ICLR_LORA = 64
GATE_LORA = 160
RWKV_PROJ = 3 * RWKV_WIDTH + DECAY_LORA + ICLR_LORA + GATE_LORA
RWKV_PROJ_PAD = 3456
LORA_OFF = 3 * RWKV_WIDTH
GN_EPS = 64e-5
MOBA_WIDTH = 1024
MOBA_HEAD_DIM = 128
MOBA_HEADS = 8
MOBA_BLOCK = 256
PAGES_PER_BLOCK = MOBA_BLOCK // PAGE_SIZE
SAMPLE_BLOCKS_PER_STEP = 4
MOBA_TOPK = 3
NUM_BUCKETS = 32
MAX_DISTANCE = 2048
D_FF = 5632
RMS_EPS = 1e-6
NEG_BIG = -1e30


def _cparams(*sem):
    return pltpu.CompilerParams(dimension_semantics=sem, vmem_limit_bytes=VMEM_LIMIT_BYTES)


def _dot(a, b):
    return jnp.dot(a, b, preferred_element_type=F32)


def _dot_nt(a, b):
    return lax.dot_general(a, b, (((1,), (1,)), ((), ())), preferred_element_type=F32)


def _dot_tn(a, b):
    return lax.dot_general(a, b, (((0,), (0,)), ((), ())), preferred_element_type=F32)


def _split2(x):
    hi = x.astype(BF16)
    lo = (x - hi.astype(F32)).astype(BF16)
    return hi, lo


def _split3(x):
    hi = x.astype(BF16)
    r1 = x - hi.astype(F32)
    mid = r1.astype(BF16)
    lo = (r1 - mid.astype(F32)).astype(BF16)
    return hi, mid, lo


def _rwkv_kernel(z_ref, shift0_ref, s0_ref, mu_ref, vecs_ref, ww_ref, wa_ref, wg_ref, seg_ref, segt_ref,
                 y_ref, sfin_ref, shift_ref, carry_z, state, *, chunk, t_valid):
    c = chunk
    ci = pl.program_id(1)
    n_chunks = pl.num_programs(1)

    hd = RWKV_HEAD_DIM

    def head_block(h):
        grp, pos = divmod(h, RWKV_GROUP)
        return grp, slice(pos * hd, (pos + 1) * hd)

    @pl.when(ci == 0)
    def _():
        carry_z[...] = shift0_ref[0]
        state[...] = jnp.zeros(state.shape, F32)
        for h in range(RWKV_HEADS):
            grp, sl = head_block(h)
            state[grp, sl, sl] = s0_ref[0, h]

    z = z_ref[0]
    row = lax.broadcasted_iota(jnp.int32, (c, 1), 0)
    z_prev = jnp.where(row == 0, carry_z[...], pltpu.roll(z, 1, axis=0))
    carry_z[...] = z[c - 1:c, :]
    last_row = (t_valid - 1) % c

    @pl.when(ci == n_chunks - 1)
    def _():
        shift_ref[0] = z[last_row:last_row + 1, :]

    zm = z + mu_ref[...] * (z_prev - z)
    r = zm[:, 0:RWKV_WIDTH]
    k = zm[:, RWKV_WIDTH:2 * RWKV_WIDTH]
    v = zm[:, 2 * RWKV_WIDTH:3 * RWKV_WIDTH]
    l0 = zm[:, LORA_OFF:LORA_OFF + LANES]
    l1 = zm[:, LORA_OFF + LANES:RWKV_PROJ_PAD]

    w0 = vecs_ref[0:1, :]
    a0 = vecs_ref[1:2, :]
    k_k = vecs_ref[2:3, :]
    k_a = vecs_ref[3:4, :]
    r_k = vecs_ref[4:5, :]
    lnx_w = vecs_ref[5:6, :]
    lnx_b = vecs_ref[6:7, :]

    wl = w0 + _dot(jnp.tanh(l0).astype(BF16), ww_ref[...])
    neg = -wl
    softplus = jnp.maximum(neg, 0.0) + jnp.log(1.0 + jnp.exp(-jnp.abs(neg)))
    logw = -jnp.exp(-softplus - 0.5)
    a = jax.nn.sigmoid(a0 + _dot(l0.astype(BF16), wa_ref[...]))
    g = _dot(jax.nn.sigmoid(l1).astype(BF16), wg_ref[...])

    seg = seg_ref[...]
    segt = segt_ref[...]

    def head_sum(x):
        hi, lo = _split2(x)
        s = _dot(hi, seg) + _dot(lo, seg)
        shi, slo = _split2(s)
        return _dot(shi, segt) + _dot(slo, segt)

    kk = k * k_k
    kk = kk / jnp.maximum(jnp.sqrt(head_sum(kk * kk)), 1e-12)
    k_h = k * (1.0 + (a - 1.0) * k_a)
    if t_valid % c != 0:
        live = (ci * c + row) < t_valid
        logw = jnp.where(live, logw, 0.0)
        kk = jnp.where(live, kk, 0.0)
        k_h = jnp.where(live, k_h, 0.0)
    b = kk * a

    ri = lax.broadcasted_iota(jnp.int32, (c, c), 0)
    cj = lax.broadcasted_iota(jnp.int32, (c, c), 1)
    tri = jnp.where(ri >= cj, 1.0, 0.0).astype(BF16)
    h1, h2, h3 = _split3(logw)
    lc = _dot(tri, h1) + _dot(tri, h2) + _dot(tri, h3)
    lc_end = lc[c - 1:c, :]
    inv_g = jnp.exp(-lc)
    r_hat = r * jnp.exp(lc)
    kap_hat = kk * jnp.exp(lc - logw)
    k_hat = k_h * inv_g
    b_hat = b * inv_g
    to_end = jnp.exp(lc_end - lc)
    k_bar = k_h * to_end
    b_bar = b * to_end
    g_end = jnp.exp(lc_end)

    gw = RWKV_GROUP * hd
    cg = RWKV_GROUP * c
    head_of_lane = lax.broadcasted_iota(jnp.int32, (1, gw), 1) // hd

    def stack(x):
        return jnp.concatenate([jnp.where(head_of_lane == j, x, 0.0) for j in range(RWKV_GROUP)],
                               axis=0).astype(BF16)

    ig = lax.broadcasted_iota(jnp.int32, (cg, cg), 0)
    jg = lax.broadcasted_iota(jnp.int32, (cg, cg), 1)
    strict = ig > jg
    lower = ig >= jg
    eye = jnp.where(ig == jg, 1.0, 0.0)

    def level(s):
        return (((ig ^ jg) & (-2 * s)) == 0) & ((ig & s) != 0) & ((jg & s) == 0)

    fused = cg % LANES == 0
    groups = range(RWKV_HEADS // RWKV_GROUP)
    lanes = [slice(grp * gw, (grp + 1) * gw) for grp in groups]
    s_old = [state[grp] for grp in groups]
    s_bf = [s.astype(BF16) for s in s_old]
    lhs = [jnp.concatenate([stack(kap_hat[:, sl]), stack(r_hat[:, sl])], axis=0) for sl in lanes]
    kh_s = [stack(k_hat[:, sl]) for sl in lanes]
    bh_s = [stack(b_hat[:, sl]) for sl in lanes]
    v_s = [stack(v[:, sl]) for sl in lanes]
    nn, mm, ark, arb = [], [], [], []
    for grp in groups:
        if fused:
            sc = _dot_nt(lhs[grp], jnp.concatenate([bh_s[grp], kh_s[grp]], axis=0))
            nn_raw, mm_raw = sc[0:cg, 0:cg], sc[0:cg, cg:2 * cg]
            arb_raw, ark_raw = sc[cg:2 * cg, 0:cg], sc[cg:2 * cg, cg:2 * cg]
        else:
            kap_s, r_s = lhs[grp][0:cg], lhs[grp][cg:2 * cg]
            nn_raw, mm_raw = _dot_nt(kap_s, bh_s[grp]), _dot_nt(kap_s, kh_s[grp])
            arb_raw, ark_raw = _dot_nt(r_s, bh_s[grp]), _dot_nt(r_s, kh_s[grp])
        nn.append(jnp.where(strict, nn_raw, 0.0))
        mm.append(jnp.where(strict, mm_raw, 0.0))
        ark.append(jnp.where(lower, ark_raw, 0.0))
        arb.append(jnp.where(lower, arb_raw, 0.0))
    from_state = [_dot_nt(lhs[grp], s_bf[grp]) for grp in groups]
    from_v = [_dot(jnp.concatenate([mm[grp], ark[grp]], axis=0).astype(BF16), v_s[grp]) for grp in groups]
    rhs = [-(from_state[grp][0:cg] + from_v[grp][0:cg]) for grp in groups]
    t_inv = [eye - jnp.where(level(1), nn[grp], 0.0) for grp in groups]
    s = 2
    while s < c:
        t_bf = [t.astype(BF16) for t in t_inv]
        half = [_dot(t_bf[grp], jnp.where(level(s), nn[grp], 0.0).astype(BF16)).astype(BF16) for grp in groups]
        t_inv = [t_inv[grp] - _dot(half[grp], t_bf[grp]) for grp in groups]
        s *= 2
    e_s = [_dot(t_inv[grp].astype(BF16), rhs[grp].astype(BF16)).astype(BF16) for grp in groups]
    y_s = [from_state[grp][cg:2 * cg] + from_v[grp][cg:2 * cg] + _dot(arb[grp].astype(BF16), e_s[grp])
           for grp in groups]
    ys = []
    for grp in groups:
        y_g = y_s[grp][0:c]
        for j in range(1, RWKV_GROUP):
            y_g = y_g + y_s[grp][j * c:(j + 1) * c]
        ys.append(y_g)
    y = jnp.concatenate(ys, axis=1)
    s_new = [s_old[grp] * g_end[:, lanes[grp]]
             + _dot_tn(jnp.concatenate([v_s[grp], e_s[grp]], axis=0),
                       jnp.concatenate([stack(k_bar[:, lanes[grp]]), stack(b_bar[:, lanes[grp]])], axis=0))
             for grp in groups]
    for grp in groups:
        state[grp] = s_new[grp]

    inv_n = 1.0 / RWKV_HEAD_DIM
    mean = head_sum(y) * inv_n
    yc = y - mean
    var = head_sum(yc * yc) * inv_n
    yn = yc * lax.rsqrt(var + GN_EPS) * lnx_w + lnx_b
    bonus = head_sum(r * k_h * r_k) * v
    y_ref[0] = ((yn + bonus) * g).astype(y_ref.dtype)

    @pl.when(ci == n_chunks - 1)
    def _():
        for h in range(RWKV_HEADS):
            grp, sl = head_block(h)
            sfin_ref[0, h] = state[grp, sl, sl]


def _rwkv_branch(z, shift0, s0, lw, *, chunk, t_valid, s0_layer=None):
    g_, t_, pr = z.shape
    nc = t_ // chunk
    kern = functools.partial(_rwkv_kernel, chunk=chunk, t_valid=t_valid)
    const2 = lambda gi, ci: (0, 0)
    state_block = (1, RWKV_HEADS, RWKV_HEAD_DIM, RWKV_HEAD_DIM)
    group_lanes = RWKV_GROUP * RWKV_HEAD_DIM
    if s0_layer is None:
        s0_spec = pl.BlockSpec(state_block, lambda gi, ci: (gi, 0, 0, 0))
    else:
        s0_spec = pl.BlockSpec((None,) + state_block, lambda gi, ci: (s0_layer, gi, 0, 0, 0))
    return pl.pallas_call(
        kern,
        grid=(g_, nc),
        in_specs=[
            pl.BlockSpec((1, chunk, pr), lambda gi, ci: (gi, ci, 0)),
            pl.BlockSpec((1, 1, pr), lambda gi, ci: (gi, 0, 0)),
            s0_spec,
            pl.BlockSpec((1, pr), const2),
            pl.BlockSpec((SUBLANES, RWKV_WIDTH), const2),
            pl.BlockSpec((LANES, RWKV_WIDTH), const2),
            pl.BlockSpec((LANES, RWKV_WIDTH), const2),
            pl.BlockSpec((2 * LANES, RWKV_WIDTH), const2),
            pl.BlockSpec((RWKV_WIDTH, LANES), const2),
            pl.BlockSpec((LANES, RWKV_WIDTH), const2),
        ],
        out_specs=[
            pl.BlockSpec((1, chunk, RWKV_WIDTH), lambda gi, ci: (gi, ci, 0)),
            pl.BlockSpec(state_block, lambda gi, ci: (gi, 0, 0, 0)),
            pl.BlockSpec((1, 1, pr), lambda gi, ci: (gi, 0, 0)),
        ],
        out_shape=[
            jax.ShapeDtypeStruct((g_, t_, RWKV_WIDTH), BF16),
            jax.ShapeDtypeStruct((g_,) + state_block[1:], F32),
            jax.ShapeDtypeStruct((g_, 1, pr), F32),
        ],
        scratch_shapes=[pltpu.VMEM((1, pr), F32),
                        pltpu.VMEM((RWKV_HEADS // RWKV_GROUP, group_lanes, group_lanes), F32)],
        compiler_params=_cparams("parallel", "arbitrary"),
        name="rwkv7_chunk",
    )(z, shift0, s0, lw["mu"], lw["vecs"], lw["ww"], lw["wa"], lw["wg"], lw["seg"], lw["segt"])


def _rwkv_weights(l, mu_shift, w0, w_lora, a0, a_lora, g_lora, k_k, k_a, r_k, lnx_w, lnx_b):
    pad = RWKV_PROJ_PAD - RWKV_PROJ
    vecs = jnp.stack([w0[l], a0[l], k_k[l], k_a[l], r_k[l].reshape(-1), lnx_w[l], lnx_b[l],
                      jnp.zeros((RWKV_WIDTH,), F32)])
    zeros64 = jnp.zeros((DECAY_LORA, RWKV_WIDTH), F32)
    head_of_lane = np.arange(RWKV_WIDTH) // RWKV_HEAD_DIM
    seg = (head_of_lane[:, None] == np.arange(LANES)[None, :]).astype(np.float32)
    return {
        "mu": jnp.pad(mu_shift[l], (0, pad))[None, :],
        "vecs": vecs,
        "ww": jnp.concatenate([w_lora[l], zeros64]).astype(BF16),
        "wa": jnp.concatenate([zeros64, a_lora[l]]).astype(BF16),
        "wg": jnp.pad(g_lora[l], ((0, 2 * LANES - GATE_LORA), (0, 0))).astype(BF16),
        "seg": jnp.asarray(seg, BF16),
        "segt": jnp.asarray(seg.T, BF16),
    }


def _ada_kernel(c_ref, w_ref, b_ref, o_ref):
    c = c_ref[...]
    act = (c * jax.nn.sigmoid(c)).astype(BF16)
    o_ref[0] = _dot(act, w_ref[0].astype(BF16)) + b_ref[0]


def _ada_mod(c_all, w_ada, b_ada, *, tn=2048):
    r_, d = c_all.shape
    l_, _, n = w_ada.shape
    return pl.pallas_call(
        _ada_kernel,
        grid=(l_, n // tn),
        in_specs=[
            pl.BlockSpec((r_, d), lambda li, j: (0, 0)),
            pl.BlockSpec((1, d, tn), lambda li, j: (li, 0, j)),
            pl.BlockSpec((1, 1, tn), lambda li, j: (li, 0, j)),
        ],
        out_specs=pl.BlockSpec((1, r_, tn), lambda li, j: (li, 0, j)),
        out_shape=jax.ShapeDtypeStruct((l_, r_, n), F32),
        compiler_params=_cparams("parallel", "parallel"),
        name="ada_mod",
    )(c_all, w_ada, b_ada.reshape(l_, 1, n))


def _mod_spec(mod, chunk, width, tt):
    per_token = mod.shape[1] != 1
    rows = tt if per_token else 1
    blocks_per_chunk = D_MODEL // width

    def index(gi, ti, *rest):
        j = rest[0] if rest else 0
        return (gi, ti if per_token else 0, chunk * blocks_per_chunk + j)

    return pl.BlockSpec((1, rows, width), index)


def _norm_kernel(x_ref, gain_ref, sc_ref, sh_ref, o_ref):
    x = x_ref[0]
    y = x * lax.rsqrt(jnp.mean(x * x, axis=-1, keepdims=True) + RMS_EPS) * gain_ref[...]
    o_ref[0] = (y * (1.0 + sc_ref[0]) + sh_ref[0]).astype(o_ref.dtype)


def _norm_mod(x, gain, mod, sh_chunk, sc_chunk, *, tt):
    g_, t_, d = x.shape
    return pl.pallas_call(
        _norm_kernel,
        grid=(g_, t_ // tt),
        in_specs=[
            pl.BlockSpec((1, tt, d), lambda gi, ti: (gi, ti, 0)),
            pl.BlockSpec((1, d), lambda gi, ti: (0, 0)),
            _mod_spec(mod, sc_chunk, d, tt),
            _mod_spec(mod, sh_chunk, d, tt),
        ],
        out_specs=pl.BlockSpec((1, tt, d), lambda gi, ti: (gi, ti, 0)),
        out_shape=jax.ShapeDtypeStruct((g_, t_, d), BF16),
        compiler_params=_cparams("parallel", "parallel"),
        name="norm_mod",
    )(x, gain.reshape(1, d), mod, mod)


def _matmul(name, lhs, ws, pairs, epilogue, out_dtypes, *, tm, tn, tiles=(), mods=(), cols=()):
    g_, t_, _ = lhs[0].shape
    w_arrays = [w[0] if isinstance(w, tuple) else w for w in ws]
    n = w_arrays[0].shape[-1]
    nl, nw, nt, nm, nc = len(lhs), len(ws), len(tiles), len(mods), len(cols)

    def w_spec(w):
        if isinstance(w, tuple):
            layer = w[1]
            return pl.BlockSpec((None, w[0].shape[1], tn), lambda gi, ti, j: (layer, 0, j))
        return pl.BlockSpec((w.shape[0], tn), lambda gi, ti, j: (0, j))

    def body(*refs):
        lhs_refs = refs[:nl]
        w_refs = refs[nl:nl + nw]
        tile_refs = refs[nl + nw:nl + nw + nt]
        mod_refs = refs[nl + nw + nt:nl + nw + nt + nm]
        col_refs = refs[nl + nw + nt + nm:nl + nw + nt + nm + nc]
        out_refs = refs[len(refs) - len(out_dtypes):]
        dots = [_dot(lhs_refs[pi][0], w_refs[i][...].astype(BF16)) for i, pi in enumerate(pairs)]
        outs = epilogue(dots, [r[0] for r in tile_refs], [r[0] for r in mod_refs], [r[...] for r in col_refs])
        for o_ref, o in zip(out_refs, outs):
            o_ref[0] = o.astype(o_ref.dtype)

    in_specs = [pl.BlockSpec((1, tm, a.shape[2]), lambda gi, ti, j: (gi, ti, 0)) for a in lhs]
    in_specs += [w_spec(w) for w in ws]
    in_specs += [pl.BlockSpec((1, tm, tn), lambda gi, ti, j: (gi, ti, j)) for _ in tiles]
    in_specs += [_mod_spec(m, chunk, tn, tm) for m, chunk in mods]
    in_specs += [pl.BlockSpec((1, tn), lambda gi, ti, j: (0, j)) for _ in cols]
    return pl.pallas_call(
        body,
        grid=(g_, t_ // tm, n // tn),
        in_specs=in_specs,
        out_specs=[pl.BlockSpec((1, tm, tn), lambda gi, ti, j: (gi, ti, j)) for _ in out_dtypes],
        out_shape=[jax.ShapeDtypeStruct((g_, t_, n), dt) for dt in out_dtypes],
        compiler_params=_cparams("parallel", "parallel", "arbitrary"),
        name=name,
    )(*lhs, *w_arrays, *tiles, *[m for m, _ in mods], *cols)


def _ep_plain(dots, tiles, mods, cols):
    return [dots[0]]


def _ep_plain2(dots, tiles, mods, cols):
    return [dots[0], dots[0]]


def _ep_head_norm(dots, tiles, mods, cols):
    x = dots[0]
    parts = []
    for j in range(x.shape[1] // MOBA_HEAD_DIM):
        xs = x[:, j * MOBA_HEAD_DIM:(j + 1) * MOBA_HEAD_DIM]
        parts.append(xs * lax.rsqrt(jnp.mean(xs * xs, axis=-1, keepdims=True) + RMS_EPS))
    y = jnp.concatenate(parts, axis=1) * cols[0]
    return [y, y]


def _ep_merge(dots, tiles, mods, cols):
    return [jax.nn.sigmoid(dots[0]) * dots[2] + jax.nn.sigmoid(dots[1]) * dots[3]]


def _ep_residual(dots, tiles, mods, cols):
    return [tiles[0] + mods[0] * dots[0]]


def _ep_swiglu(dots, tiles, mods, cols):
    gate = dots[0]
    return [gate * jax.nn.sigmoid(gate) * dots[1]]


def _t5_bucket(dist):
    n = jnp.maximum(dist, 0)
    max_exact = NUM_BUCKETS // 2
    nf = jnp.maximum(n, 1).astype(F32)
    large = max_exact + (jnp.log(nf / max_exact) / math.log(MAX_DISTANCE / max_exact)
                         * (NUM_BUCKETS - max_exact)).astype(jnp.int32)
    large = jnp.minimum(large, NUM_BUCKETS - 1)
    return jnp.where(n < max_exact, n, large)


def _top_blocks(gates, lane_f):
    sel = jnp.zeros_like(gates)
    for _ in range(MOBA_TOPK):
        mx = jnp.max(gates, axis=-1, keepdims=True)
        first = jnp.min(jnp.where(gates == mx, lane_f, float(LANES)), axis=-1, keepdims=True)
        pick = (lane_f == first) & (mx > 0.5 * NEG_BIG)
        sel = jnp.where(pick, 1.0, sel)
        gates = jnp.where(pick, NEG_BIG, gates)
    return sel


def _kmean_kernel(k_ref, o_ref):
    n = pl.program_id(1)
    o_ref[0, pl.ds(n, 1), :] = jnp.mean(k_ref[0], axis=0, keepdims=True)


def _block_means(k):
    b_, t_, _ = k.shape
    nb = t_ // MOBA_BLOCK
    return pl.pallas_call(
        _kmean_kernel,
        grid=(b_, nb),
        in_specs=[pl.BlockSpec((1, MOBA_BLOCK, MOBA_WIDTH), lambda bi, n: (bi, n, 0))],
        out_specs=pl.BlockSpec((1, nb, MOBA_WIDTH), lambda bi, n: (bi, 0, 0)),
        out_shape=jax.ShapeDtypeStruct((b_, nb, MOBA_WIDTH), F32),
        compiler_params=_cparams("parallel", "arbitrary"),
        name="moba_block_means",
    )(k)


def _bias_tile_kernel(rt_ref, o_ref, *, n_blocks):
    delta = pl.program_id(1)
    blk = MOBA_BLOCK
    start = pl.multiple_of((n_blocks - 1 - delta) * blk, blk)
    window = rt_ref[0, :, pl.ds(start, 2 * blk)]
    rolled = pltpu.roll(jnp.broadcast_to(window, (blk, 2 * blk)), 0, 1, stride=1, stride_axis=0)
    r_i = lax.broadcasted_iota(jnp.int32, (blk, blk), 0)
    c_i = lax.broadcasted_iota(jnp.int32, (blk, blk), 1)
    on_diag = (delta == 0).astype(jnp.int32)
    o_ref[0, 0] = jnp.where((c_i - r_i) * on_diag <= 0, rolled[:, blk:2 * blk], NEG_BIG)


def _bias_tiles(rev_bias, n_blocks):
    h_ = rev_bias.shape[0]
    blk = MOBA_BLOCK
    return pl.pallas_call(
        functools.partial(_bias_tile_kernel, n_blocks=n_blocks),
        grid=(h_, n_blocks),
        in_specs=[pl.BlockSpec((1, 1, rev_bias.shape[2]), lambda h, dl: (h, 0, 0))],
        out_specs=pl.BlockSpec((1, 1, blk, blk), lambda h, dl: (h, dl, 0, 0)),
        out_shape=jax.ShapeDtypeStruct((h_, n_blocks, blk, blk), F32),
        compiler_params=_cparams("parallel", "parallel"),
        name="moba_bias_tiles",
    )(rev_bias)


def _moba_prompt_kernel(q_ref, k_ref, v_ref, km_ref, bias_ref, o_ref, qa_sc, m_sc, l_sc, acc_sc, *, n_blocks):
    i = pl.program_id(2)
    blk = MOBA_BLOCK
    nb = n_blocks
    q = q_ref[0]
    km1, km2, km3 = _split3(km_ref[0])
    q1, q2, q3 = _split3(q)
    g1 = _dot_nt(jnp.concatenate([km1, km2, km3], axis=0), q1)
    g2 = _dot_nt(jnp.concatenate([km1, km2], axis=0), q2)
    g3 = _dot_nt(km1, q3)
    gates = (g1[2 * nb:3 * nb] + g2[nb:2 * nb] + g3) + (g1[nb:2 * nb] + g2[0:nb]) + g1[0:nb]
    n_i = lax.broadcasted_iota(jnp.int32, (nb, blk), 0)
    n_f = n_i.astype(F32)
    gates = jnp.where(n_i < i, gates, NEG_BIG)
    sel = jnp.where(n_i == i, 1.0, 0.0)
    for _ in range(MOBA_TOPK):
        mx = jnp.max(gates, axis=0, keepdims=True)
        first = jnp.min(jnp.where(gates == mx, n_f, float(nb)), axis=0, keepdims=True)
        pick = (n_f == first) & (mx > 0.5 * NEG_BIG)
        sel = jnp.where(pick, 1.0, sel)
        gates = jnp.where(pick, NEG_BIG, gates)
    hidden = jnp.where(sel > 0.0, 0.0, NEG_BIG)
    hidden = jnp.concatenate([hidden, jnp.zeros((LANES - nb, blk), F32)], axis=0).T
    qa_sc[...] = jnp.concatenate([(q * (MOBA_HEAD_DIM ** -0.5)).astype(BF16), hidden.astype(BF16)], axis=1)
    m_sc[...] = jnp.full(m_sc.shape, NEG_BIG, F32)
    l_sc[...] = jnp.zeros(l_sc.shape, F32)
    acc_sc[...] = jnp.zeros(acc_sc.shape, F32)
    lane_i = lax.broadcasted_iota(jnp.int32, (blk, LANES), 1)

    def logits_of_pair(jj):
        qa = qa_sc[...]
        parts = []
        for j_raw in (2 * jj, 2 * jj + 1):
            j = jnp.minimum(j_raw, nb - 1)
            rows = pl.ds(pl.multiple_of(j * blk, blk), blk)
            k_aug = jnp.concatenate([k_ref[0, rows, :], jnp.where(lane_i == j_raw, 1.0, 0.0).astype(BF16)], axis=1)
            parts.append(_dot_nt(qa, k_aug) + bias_ref[0, jnp.clip(i - j_raw, 0, nb - 1)])
        return jnp.concatenate(parts, axis=1)

    def pair(jj, s_cur):
        s_next = logits_of_pair(jj + 1)
        vs = [v_ref[0, pl.ds(pl.multiple_of(j * blk, blk), blk), :] for j in (2 * jj, 2 * jj + 1)]
        cols = [s_cur[:, x * LANES:(x + 1) * LANES] for x in range(2 * blk // LANES)]
        top = cols[0]
        for x in cols[1:]:
            top = jnp.maximum(top, x)
        m_prev = m_sc[...]
        m_new = jnp.maximum(m_prev, jnp.max(top, axis=-1, keepdims=True))
        ps = [jnp.exp(x - m_new) for x in cols]
        tot = ps[0]
        for x in ps[1:]:
            tot = tot + x
        alpha = jnp.exp(m_prev - m_new)
        l_sc[...] = alpha * l_sc[...] + jnp.sum(tot, axis=-1, keepdims=True)
        p_all = jnp.concatenate(ps, axis=1).astype(BF16)
        acc_sc[...] = alpha * acc_sc[...] + _dot(p_all, jnp.concatenate(vs, axis=0))
        m_sc[...] = m_new
        return s_next

    lax.fori_loop(0, (i + 2) // 2, pair, logits_of_pair(0))
    o_ref[0] = (acc_sc[...] / l_sc[...]).astype(o_ref.dtype)


def _moba_prompt(q, k_bf, v_bf, kmean, bias_tiles):
    b_, t_, _ = q.shape
    nb = t_ // MOBA_BLOCK
    h_ = MOBA_HEADS
    blk = MOBA_BLOCK
    return pl.pallas_call(
        functools.partial(_moba_prompt_kernel, n_blocks=nb),
        grid=(b_, h_, nb),
        in_specs=[
            pl.BlockSpec((1, blk, LANES), lambda bi, h, i: (bi, i, h)),
            pl.BlockSpec((1, t_, LANES), lambda bi, h, i: (bi, 0, h)),
            pl.BlockSpec((1, t_, LANES), lambda bi, h, i: (bi, 0, h)),
            pl.BlockSpec((1, nb, LANES), lambda bi, h, i: (bi, 0, h)),
            pl.BlockSpec((1, nb, blk, blk), lambda bi, h, i: (h, 0, 0, 0)),
        ],
        out_specs=pl.BlockSpec((1, blk, LANES), lambda bi, h, i: (bi, i, h)),
        out_shape=jax.ShapeDtypeStruct((b_, t_, MOBA_WIDTH), BF16),
        scratch_shapes=[pltpu.VMEM((blk, 2 * LANES), BF16), pltpu.VMEM((blk, LANES), F32),
                        pltpu.VMEM((blk, LANES), F32), pltpu.VMEM((blk, LANES), F32)],
        compiler_params=_cparams("parallel", "parallel", "arbitrary"),
        name="moba_prompt",
    )(q, k_bf, v_bf, kmean, bias_tiles)


def _prompt_bias_table(rel_bias, t_):
    y = jnp.arange(t_ + MOBA_BLOCK, dtype=jnp.int32)
    tab = rel_bias[_t5_bucket(t_ - y)]
    tab = jnp.where((y <= t_)[:, None], tab, 0.0)
    return tab.T.reshape(MOBA_HEADS, 1, t_ + MOBA_BLOCK)


def _moba_sample_kernel(pt_ref, q_ref, *refs, n_past):
    n_pages = SAMPLE_BLOCKS_PER_STEP * PAGES_PER_BLOCK
    k_refs, v_refs = refs[:n_pages], refs[n_pages:2 * n_pages]
    kn_ref, vn_ref, bias_ref, ownb_ref, o_ref, m_sc, l_sc, g_sc, acc_sc = refs[2 * n_pages:]
    step = pl.program_id(1)
    scale = MOBA_HEAD_DIM ** -0.5
    rows_n = q_ref.shape[1]
    cols_n = MOBA_BLOCK * MOBA_HEADS
    q = q_ref[0]
    q_bf = (q * scale).astype(BF16)
    shape = (rows_n, LANES)
    for blk in range(SAMPLE_BLOCKS_PER_STEP):
        n = step * SAMPLE_BLOCKS_PER_STEP + blk
        pages = slice(blk * PAGES_PER_BLOCK, (blk + 1) * PAGES_PER_BLOCK)
        kf = jnp.concatenate([r_[...] for r_ in k_refs[pages]], axis=0)
        vf = jnp.concatenate([r_[...] for r_ in v_refs[pages]], axis=0)
        logits = _dot_nt(q_bf, kf.astype(BF16)) + bias_ref[:, pl.ds(pl.multiple_of(n * cols_n, cols_n), cols_n)]
        m_b = jnp.max(logits, axis=-1, keepdims=True)
        p = jnp.exp(logits - m_b)
        m_sc[n] = jnp.broadcast_to(m_b, shape)
        l_sc[n] = jnp.broadcast_to(jnp.sum(p, axis=-1, keepdims=True), shape)
        acc_sc[n] = _dot(p.astype(BF16), vf.astype(BF16))
        kmean = jnp.mean(kf.reshape(MOBA_BLOCK, MOBA_HEADS, MOBA_HEAD_DIM), axis=0)
        gate = jnp.concatenate([jnp.sum(q[t * MOBA_HEADS:(t + 1) * MOBA_HEADS] * kmean, axis=-1, keepdims=True)
                                for t in range(rows_n // MOBA_HEADS)], axis=0)
        g_sc[n] = jnp.broadcast_to(gate, shape)

    @pl.when(step == pl.num_programs(1) - 1)
    def _():
        lane_f = lax.broadcasted_iota(jnp.int32, (1, LANES), 1).astype(F32)
        gates = jnp.full(shape, NEG_BIG, F32)
        for nb in range(n_past):
            gates = jnp.where(lane_f == float(nb), g_sc[nb], gates)
        sel = _top_blocks(gates, lane_f)
        s_own = _dot_nt(q_bf, kn_ref[0].astype(BF16)) + ownb_ref[...]
        m_tot = jnp.broadcast_to(jnp.max(s_own, axis=-1, keepdims=True), shape)
        picked = []
        for nb in range(n_past):
            on = jnp.max(jnp.where(lane_f == float(nb), sel, 0.0), axis=-1, keepdims=True) > 0.0
            picked.append(on)
            m_tot = jnp.maximum(m_tot, jnp.where(on, m_sc[nb], NEG_BIG))
        p_own = jnp.exp(s_own - m_tot[:, 0:rows_n])
        l_tot = jnp.broadcast_to(jnp.sum(p_own, axis=-1, keepdims=True), shape)
        acc = _dot(p_own.astype(BF16), vn_ref[0].astype(BF16))
        for nb in range(n_past):
            w = jnp.where(picked[nb], jnp.exp(m_sc[nb] - m_tot), 0.0)
            l_tot = l_tot + w * l_sc[nb]
            acc = acc + w * acc_sc[nb]
        o_ref[0] = (acc / l_tot).astype(o_ref.dtype)


def _moba_sample(layer, page_table, q, k_new, v_new, cache_k, cache_v, bias_tab, own_bias):
    s_, rows_n, _ = q.shape
    n_past = page_table.shape[1] * PAGE_SIZE // MOBA_BLOCK
    row = pl.BlockSpec((1, rows_n, MOBA_HEAD_DIM), lambda si, n, pt: (si, 0, 0))

    pages_per_step = SAMPLE_BLOCKS_PER_STEP * PAGES_PER_BLOCK

    def page_spec(pg):
        return pl.BlockSpec((None, None, PAGE_SIZE * MOBA_HEADS, MOBA_HEAD_DIM),
                            lambda si, n, pt: (layer, pt[si, pages_per_step * n + pg], 0, 0))

    tiles = [page_spec(pg) for pg in range(pages_per_step)]
    rows_shape = cache_k.shape[:2] + (PAGE_SIZE * MOBA_HEADS, MOBA_HEAD_DIM)
    cache_k = cache_k.reshape(rows_shape)
    cache_v = cache_v.reshape(rows_shape)
    grid_spec = pltpu.PrefetchScalarGridSpec(
        num_scalar_prefetch=1,
        grid=(s_, n_past // SAMPLE_BLOCKS_PER_STEP),
        in_specs=[row] + tiles + tiles + [
            row,
            row,
            pl.BlockSpec(bias_tab.shape, lambda si, n, pt: (0, 0)),
            pl.BlockSpec(own_bias.shape, lambda si, n, pt: (0, 0)),
        ],
        out_specs=row,
        scratch_shapes=[pltpu.VMEM((n_past, rows_n, LANES), F32) for _ in range(4)],
    )
    return pl.pallas_call(
        functools.partial(_moba_sample_kernel, n_past=n_past),
        grid_spec=grid_spec,
        out_shape=jax.ShapeDtypeStruct((s_, rows_n, MOBA_HEAD_DIM), BF16),
        compiler_params=_cparams("parallel", "arbitrary"),
        name="moba_sample",
    )(page_table, q, *([cache_k] * len(tiles)), *([cache_v] * len(tiles)), k_new, v_new, bias_tab, own_bias)


def _sample_bias_tables(rel_bias, past_len, t_new):
    h_ = rel_bias.shape[1]
    same_head = jnp.eye(h_, dtype=bool)
    dist = past_len + t_new - 1 - jnp.arange(past_len + t_new, dtype=jnp.int32)
    by_dist_rev = rel_bias[_t5_bucket(dist)]
    tab = jnp.stack([by_dist_rev[t_new - 1 - t:t_new - 1 - t + past_len] for t in range(t_new)])
    per_key = jnp.transpose(tab, (0, 2, 1)).reshape(t_new * h_, past_len)
    t = jnp.arange(t_new, dtype=jnp.int32)
    own = rel_bias[_t5_bucket(t[:, None] - t[None, :])]
    own = jnp.where((t[None, :] <= t[:, None])[..., None], own, NEG_BIG)
    own = jnp.where(same_head[None, :, None, :], jnp.transpose(own, (0, 2, 1))[..., None], NEG_BIG)
    return _spread_over_heads(per_key, h_), own.reshape(t_new * h_, t_new * h_)


def _spread_kernel(x_ref, e_ref, neg_ref, o_ref):
    hi, mid, lo = _split3(x_ref[...])
    e = e_ref[...]
    o_ref[...] = (_dot(hi, e) + _dot(mid, e) + _dot(lo, e)) + neg_ref[...]


def _spread_over_heads(per_key, h_):
    rows_n, n_keys = per_key.shape
    blk = MOBA_BLOCK
    spread = np.zeros((blk, blk * h_), np.float32)
    spread[np.arange(blk * h_) // h_, np.arange(blk * h_)] = 1.0
    other_head = np.where((np.arange(rows_n) % h_)[:, None] == (np.arange(blk * h_) % h_)[None, :], 0.0, NEG_BIG)
    return pl.pallas_call(
        _spread_kernel,
        grid=(n_keys // blk,),
        in_specs=[
            pl.BlockSpec((rows_n, blk), lambda n: (0, n)),
            pl.BlockSpec(spread.shape, lambda n: (0, 0)),
            pl.BlockSpec(other_head.shape, lambda n: (0, 0)),
        ],
        out_specs=pl.BlockSpec((rows_n, blk * h_), lambda n: (0, n)),
        out_shape=jax.ShapeDtypeStruct((rows_n, n_keys * h_), F32),
        compiler_params=_cparams("parallel"),
        name="moba_sample_bias",
    )(per_key, jnp.asarray(spread, BF16), jnp.asarray(other_head, F32))


PROMPT_ROW_TILE = 1024

SH1, SC1, G1, SH2, SC2, G2 = range(6)


def _layer_weights(l, w_in, q_norm, k_norm, w_up_a, w_up_b, w_out, w_ffn_gate, w_ffn_up, w_ffn_down):
    o_q = RWKV_PROJ
    o_v = o_q + 2 * MOBA_WIDTH
    o_ga = o_v + MOBA_WIDTH
    o_gb = o_ga + D_MODEL
    wl = w_in[l]
    heads = MOBA_WIDTH // MOBA_HEAD_DIM
    return {
        "rwkv": jnp.pad(wl[:, :RWKV_PROJ], ((0, 0), (0, RWKV_PROJ_PAD - RWKV_PROJ))).astype(BF16),
        "q": wl[:, o_q:o_q + MOBA_WIDTH].astype(BF16),
        "k": wl[:, o_q + MOBA_WIDTH:o_v].astype(BF16),
        "v": wl[:, o_v:o_ga].astype(BF16),
        "ga": wl[:, o_ga:o_gb].astype(BF16),
        "gb": wl[:, o_gb:].astype(BF16),
        "q_gain": jnp.tile(q_norm[l], heads)[None, :],
        "k_gain": jnp.tile(k_norm[l], heads)[None, :],
        "up_a": (w_up_a, l),
        "up_b": (w_up_b, l),
        "out": w_out[l].astype(BF16),
        "fg": (w_ffn_gate, l),
        "fu": (w_ffn_up, l),
        "fd": (w_ffn_down, l),
    }


def _trunk_layer(x, mod, norm_mix, norm_ffn, w, rwkv_fn, attn_fn, *, tm):
    h = _norm_mod(x, norm_mix, mod, SH1, SC1, tt=min(tm, 512))
    z_r = _matmul("in_rwkv", [h], [w["rwkv"]], [0], _ep_plain, [F32], tm=tm, tn=RWKV_PROJ_PAD // 3)[0]
    q = _matmul("in_q", [h], [w["q"]], [0], _ep_head_norm, [F32], tm=tm, tn=1024, cols=[w["q_gain"]])[0]
    k, k_bf = _matmul("in_k", [h], [w["k"]], [0], _ep_head_norm, [F32, BF16], tm=tm, tn=1024, cols=[w["k_gain"]])
    v, v_bf = _matmul("in_v", [h], [w["v"]], [0], _ep_plain2, [F32, BF16], tm=tm, tn=1024)
    y_a, s_new, shift_new = rwkv_fn(z_r)
    y_b = attn_fn(q, k, k_bf, v, v_bf)
    merged = _matmul("merge", [h, y_a, y_b], [w["ga"], w["gb"], w["up_a"], w["up_b"]], [0, 0, 1, 2],
                     _ep_merge, [BF16], tm=tm, tn=512)[0]
    x1 = _matmul("out_proj", [merged], [w["out"]], [0], _ep_residual, [F32], tm=min(2 * tm, x.shape[1]), tn=512,
                 tiles=[x], mods=[(mod, G1)])[0]
    h2 = _norm_mod(x1, norm_ffn, mod, SH2, SC2, tt=min(tm, 512))
    act = _matmul("ffn_in", [h2], [w["fg"], w["fu"]], [0, 0], _ep_swiglu, [BF16], tm=tm, tn=512)[0]
    x2 = _matmul("ffn_out", [act], [w["fd"]], [0], _ep_residual, [F32], tm=tm, tn=256,
                 tiles=[x1], mods=[(mod, G2)])[0]
    return x2, k, v, s_new, shift_new


def kernel(x_prompt, x_sample, cache_k, cache_v, state_wkv, state_shift, page_table, c_prompt, c_sample, rel_bias, w_ada, b_ada, norm_mix, norm_ffn, w_in, mu_shift, w0, w_lora, a0, a_lora, g_lora, k_k, k_a, r_k, lnx_w, lnx_b, q_norm, k_norm, w_up_a, w_up_b, w_out, w_ffn_gate, w_ffn_up, w_ffn_down):
    bp, tp, d = x_prompt.shape
    bs, ts, _ = x_sample.shape
    n_seq = bp + bs
    c_all = jnp.concatenate([c_prompt, c_sample])
    c_all = jnp.pad(c_all, ((0, -n_seq % SUBLANES), (0, 0)))
    mod = _ada_mod(c_all, w_ada, b_ada)

    past_len = page_table.shape[1] * PAGE_SIZE
    bias_tiles = _bias_tiles(_prompt_bias_table(rel_bias, tp), tp // MOBA_BLOCK)
    bias_tab, own_bias = _sample_bias_tables(rel_bias, past_len, ts)
    pr_pad = RWKV_PROJ_PAD - RWKV_PROJ
    sample_chunk = SUBLANES

    def sample_rows(a):
        return a.reshape(bs, ts * MOBA_HEADS, MOBA_HEAD_DIM)

    x_p = x_prompt
    x_s = x_sample.reshape(1, bs * ts, d)
    outs = [[] for _ in range(8)]
    for l in range(DEPTH):
        w = _layer_weights(l, w_in, q_norm, k_norm, w_up_a, w_up_b, w_out, w_ffn_gate, w_ffn_up, w_ffn_down)
        lw = _rwkv_weights(l, mu_shift, w0, w_lora, a0, a_lora, g_lora, k_k, k_a, r_k, lnx_w, lnx_b)
        mod_p = mod[l, :bp][:, None, :]
        mod_s = jnp.repeat(mod[l, bp:n_seq], ts, axis=0)[None]

        def rwkv_p(z_r):
            shift0 = jnp.zeros((bp, 1, RWKV_PROJ_PAD), F32)
            s0 = jnp.zeros((bp, RWKV_HEADS, RWKV_HEAD_DIM, RWKV_HEAD_DIM), F32)
            return _rwkv_branch(z_r, shift0, s0, lw, chunk=64, t_valid=tp)

        def attn_p(q, k, k_bf, v, v_bf):
            return _moba_prompt(q, k_bf, v_bf, _block_means(k), bias_tiles)

        x_p, k_p, v_p, s_p, sh_p = _trunk_layer(x_p, mod_p, norm_mix[l], norm_ffn[l], w, rwkv_p, attn_p,
                                                 tm=min(tp, PROMPT_ROW_TILE))

        def rwkv_s(z_r):
            z = z_r.reshape(bs, ts, RWKV_PROJ_PAD)
            z = jnp.pad(z, ((0, 0), (0, sample_chunk - ts), (0, 0)))
            shift0 = jnp.pad(state_shift[l], ((0, 0), (0, pr_pad)))[:, None, :]
            y, s_new, shift_new = _rwkv_branch(z, shift0, state_wkv, lw, chunk=sample_chunk, t_valid=ts,
                                               s0_layer=l)
            return y[:, :ts].reshape(1, bs * ts, RWKV_WIDTH), s_new, shift_new

        def attn_s(q, k, k_bf, v, v_bf):
            y = _moba_sample(l, page_table, sample_rows(q), sample_rows(k), sample_rows(v), cache_k, cache_v,
                             bias_tab, own_bias)
            return y.reshape(1, bs * ts, MOBA_WIDTH)

        x_s, k_s, v_s, s_s, sh_s = _trunk_layer(x_s, mod_s, norm_mix[l], norm_ffn[l], w, rwkv_s, attn_s,
                                                 tm=bs * ts)

        kv_shape_p = (bp, tp, MOBA_HEADS, MOBA_HEAD_DIM)
        kv_shape_s = (bs, ts, MOBA_HEADS, MOBA_HEAD_DIM)
        for lst, val in zip(outs, [
                k_p.reshape(kv_shape_p), v_p.reshape(kv_shape_p),
                s_p, sh_p[:, 0, :RWKV_PROJ],
                k_s.reshape(kv_shape_s), v_s.reshape(kv_shape_s),
                s_s, sh_s[:, 0, :RWKV_PROJ]]):
            lst.append(val)
    return (x_p, x_s.reshape(bs, ts, d)) + tuple(jnp.stack(o) for o in outs)
```

```python
import functools
import math

import jax
import jax.numpy as jnp
import numpy as np
from jax import lax
from jax.experimental import pallas as pl
from jax.experimental.pallas import tpu as pltpu

F32 = jnp.float32
BF16 = jnp.bfloat16

LANES = 128
SUBLANES = 8
VMEM_LIMIT_BYTES = 56 * 1024 * 1024

D_MODEL = 2048
DEPTH = 2
PAGE_SIZE = 128
RWKV_WIDTH = 1024
RWKV_HEAD_DIM = 64
RWKV_HEADS = 16
RWKV_GROUP = 4
DECAY_LORA = 64
ICLR_LORA = 64
GATE_LORA = 160
RWKV_PROJ = 3 * RWKV_WIDTH + DECAY_LORA + ICLR_LORA + GATE_LORA
RWKV_PROJ_PAD = 3456
LORA_OFF = 3 * RWKV_WIDTH
GN_EPS = 64e-5
MOBA_WIDTH = 1024
MOBA_HEAD_DIM = 128
MOBA_HEADS = 8
MOBA_BLOCK = 256
PAGES_PER_BLOCK = MOBA_BLOCK // PAGE_SIZE
SAMPLE_BLOCKS_PER_STEP = 8
MOBA_TOPK = 3
NUM_BUCKETS = 32
MAX_DISTANCE = 2048
D_FF = 5632
RMS_EPS = 1e-6
NEG_BIG = -1e30


def _cparams(*sem):
    return pltpu.CompilerParams(dimension_semantics=sem, vmem_limit_bytes=VMEM_LIMIT_BYTES)


def _dot(a, b):
    return jnp.dot(a, b, preferred_element_type=F32)


def _dot_nt(a, b):
    return lax.dot_general(a, b, (((1,), (1,)), ((), ())), preferred_element_type=F32)


def _dot_tn(a, b):
    return lax.dot_general(a, b, (((0,), (0,)), ((), ())), preferred_element_type=F32)


def _split2(x):
    hi = x.astype(BF16)
    lo = (x - hi.astype(F32)).astype(BF16)
    return hi, lo


def _split3(x):
    hi = x.astype(BF16)
    r1 = x - hi.astype(F32)
    mid = r1.astype(BF16)
    lo = (r1 - mid.astype(F32)).astype(BF16)
    return hi, mid, lo


def _rwkv_kernel(z_ref, shift0_ref, s0_ref, mu_ref, vecs_ref, ww_ref, wa_ref, wg_ref, seg_ref, segt_ref,
                 y_ref, sfin_ref, shift_ref, carry_z, state, *, chunk, t_valid):
    c = chunk
    ci = pl.program_id(1)
    n_chunks = pl.num_programs(1)

    hd = RWKV_HEAD_DIM

    def head_block(h):
        grp, pos = divmod(h, RWKV_GROUP)
        return grp, slice(pos * hd, (pos + 1) * hd)

    @pl.when(ci == 0)
    def _():
        carry_z[...] = shift0_ref[0]
        state[...] = jnp.zeros(state.shape, F32)
        for h in range(RWKV_HEADS):
            grp, sl = head_block(h)
            state[grp, sl, sl] = s0_ref[0, h]

    z = z_ref[0]
    row = lax.broadcasted_iota(jnp.int32, (c, 1), 0)
    z_prev = jnp.where(row == 0, carry_z[...], pltpu.roll(z, 1, axis=0))
    carry_z[...] = z[c - 1:c, :]
    last_row = (t_valid - 1) % c

    @pl.when(ci == n_chunks - 1)
    def _():
        shift_ref[0] = z[last_row:last_row + 1, :]

    zm = z + mu_ref[...] * (z_prev - z)
    r = zm[:, 0:RWKV_WIDTH]
    k = zm[:, RWKV_WIDTH:2 * RWKV_WIDTH]
    v = zm[:, 2 * RWKV_WIDTH:3 * RWKV_WIDTH]
    l0 = zm[:, LORA_OFF:LORA_OFF + LANES]
    l1 = zm[:, LORA_OFF + LANES:RWKV_PROJ_PAD]

    w0 = vecs_ref[0:1, :]
    a0 = vecs_ref[1:2, :]
    k_k = vecs_ref[2:3, :]
    k_a = vecs_ref[3:4, :]
    r_k = vecs_ref[4:5, :]
    lnx_w = vecs_ref[5:6, :]
    lnx_b = vecs_ref[6:7, :]

    wl = w0 + _dot(jnp.tanh(l0).astype(BF16), ww_ref[...])
    neg = -wl
    softplus = jnp.maximum(neg, 0.0) + jnp.log(1.0 + jnp.exp(-jnp.abs(neg)))
    logw = -jnp.exp(-softplus - 0.5)
    a = jax.nn.sigmoid(a0 + _dot(l0.astype(BF16), wa_ref[...]))
    g = _dot(jax.nn.sigmoid(l1).astype(BF16), wg_ref[...])

    seg = seg_ref[...]
    segt = segt_ref[...]

    def head_sum(x):
        hi, lo = _split2(x)
        s = _dot(hi, seg) + _dot(lo, seg)
        shi, slo = _split2(s)
        return _dot(shi, segt) + _dot(slo, segt)

    kk = k * k_k
    kk = kk / jnp.maximum(jnp.sqrt(head_sum(kk * kk)), 1e-12)
    k_h = k * (1.0 + (a - 1.0) * k_a)
    if t_valid % c != 0:
        live = (ci * c + row) < t_valid
        logw = jnp.where(live, logw, 0.0)
        kk = jnp.where(live, kk, 0.0)
        k_h = jnp.where(live, k_h, 0.0)
    b = kk * a

    ri = lax.broadcasted_iota(jnp.int32, (c, c), 0)
    cj = lax.broadcasted_iota(jnp.int32, (c, c), 1)
    tri = jnp.where(ri >= cj, 1.0, 0.0).astype(BF16)
    h1, h2, h3 = _split3(logw)
    lc = _dot(tri, h1) + _dot(tri, h2) + _dot(tri, h3)
    lc_end = lc[c - 1:c, :]
    inv_g = jnp.exp(-lc)
    r_hat = r * jnp.exp(lc)
    kap_hat = kk * jnp.exp(lc - logw)
    k_hat = k_h * inv_g
    b_hat = b * inv_g
    to_end = jnp.exp(lc_end - lc)
    k_bar = k_h * to_end
    b_bar = b * to_end
    g_end = jnp.exp(lc_end)

    gw = RWKV_GROUP * hd
    cg = RWKV_GROUP * c
    head_of_lane = lax.broadcasted_iota(jnp.int32, (1, gw), 1) // hd

    def stack(x):
        return jnp.concatenate([jnp.where(head_of_lane == j, x, 0.0) for j in range(RWKV_GROUP)],
                               axis=0).astype(BF16)

    ig = lax.broadcasted_iota(jnp.int32, (cg, cg), 0)
    jg = lax.broadcasted_iota(jnp.int32, (cg, cg), 1)
    strict = ig > jg
    lower = ig >= jg
    eye = jnp.where(ig == jg, 1.0, 0.0)

    def level(s):
        return (((ig ^ jg) & (-2 * s)) == 0) & ((ig & s) != 0) & ((jg & s) == 0)

    fused = cg % LANES == 0
    groups = range(RWKV_HEADS // RWKV_GROUP)
    lanes = [slice(grp * gw, (grp + 1) * gw) for grp in groups]
    s_old = [state[grp] for grp in groups]
    s_bf = [s.astype(BF16) for s in s_old]
    lhs = [jnp.concatenate([stack(kap_hat[:, sl]), stack(r_hat[:, sl])], axis=0) for sl in lanes]
    kh_s = [stack(k_hat[:, sl]) for sl in lanes]
    bh_s = [stack(b_hat[:, sl]) for sl in lanes]
    v_s = [stack(v[:, sl]) for sl in lanes]
    nn, mm, ark, arb = [], [], [], []
    for grp in groups:
        if fused:
            sc = _dot_nt(lhs[grp], jnp.concatenate([bh_s[grp], kh_s[grp]], axis=0))
            nn_raw, mm_raw = sc[0:cg, 0:cg], sc[0:cg, cg:2 * cg]
            arb_raw, ark_raw = sc[cg:2 * cg, 0:cg], sc[cg:2 * cg, cg:2 * cg]
        else:
            kap_s, r_s = lhs[grp][0:cg], lhs[grp][cg:2 * cg]
            nn_raw, mm_raw = _dot_nt(kap_s, bh_s[grp]), _dot_nt(kap_s, kh_s[grp])
            arb_raw, ark_raw = _dot_nt(r_s, bh_s[grp]), _dot_nt(r_s, kh_s[grp])
        nn.append(jnp.where(strict, nn_raw, 0.0))
        mm.append(jnp.where(strict, mm_raw, 0.0))
        ark.append(jnp.where(lower, ark_raw, 0.0))
        arb.append(jnp.where(lower, arb_raw, 0.0))
    from_state = [_dot_nt(lhs[grp], s_bf[grp]) for grp in groups]
    from_v = [_dot(jnp.concatenate([mm[grp], ark[grp]], axis=0).astype(BF16), v_s[grp]) for grp in groups]
    rhs = [-(from_state[grp][0:cg] + from_v[grp][0:cg]) for grp in groups]
    t_inv = [eye - jnp.where(level(1), nn[grp], 0.0) for grp in groups]
    s = 2
    while s < c:
        t_bf = [t.astype(BF16) for t in t_inv]
        half = [_dot(t_bf[grp], jnp.where(level(s), nn[grp], 0.0).astype(BF16)).astype(BF16) for grp in groups]
        t_inv = [t_inv[grp] - _dot(half[grp], t_bf[grp]) for grp in groups]
        s *= 2
    e_s = [_dot(t_inv[grp].astype(BF16), rhs[grp].astype(BF16)).astype(BF16) for grp in groups]
    y_s = [from_state[grp][cg:2 * cg] + from_v[grp][cg:2 * cg] + _dot(arb[grp].astype(BF16), e_s[grp])
           for grp in groups]
    ys = []
    for grp in groups:
        y_g = y_s[grp][0:c]
        for j in range(1, RWKV_GROUP):
            y_g = y_g + y_s[grp][j * c:(j + 1) * c]
        ys.append(y_g)
    y = jnp.concatenate(ys, axis=1)
    s_new = [s_old[grp] * g_end[:, lanes[grp]]
             + _dot_tn(jnp.concatenate([v_s[grp], e_s[grp]], axis=0),
                       jnp.concatenate([stack(k_bar[:, lanes[grp]]), stack(b_bar[:, lanes[grp]])], axis=0))
             for grp in groups]
    for grp in groups:
        state[grp] = s_new[grp]

    inv_n = 1.0 / RWKV_HEAD_DIM
    mean = head_sum(y) * inv_n
    yc = y - mean
    var = head_sum(yc * yc) * inv_n
    yn = yc * lax.rsqrt(var + GN_EPS) * lnx_w + lnx_b
    bonus = head_sum(r * k_h * r_k) * v
    y_ref[0] = ((yn + bonus) * g).astype(y_ref.dtype)

    @pl.when(ci == n_chunks - 1)
    def _():
        for h in range(RWKV_HEADS):
            grp, sl = head_block(h)
            sfin_ref[0, h] = state[grp, sl, sl]


def _rwkv_branch(z, shift0, s0, lw, *, chunk, t_valid, s0_layer=None):
    g_, t_, pr = z.shape
    nc = t_ // chunk
    kern = functools.partial(_rwkv_kernel, chunk=chunk, t_valid=t_valid)
    const2 = lambda gi, ci: (0, 0)
    state_block = (1, RWKV_HEADS, RWKV_HEAD_DIM, RWKV_HEAD_DIM)
    group_lanes = RWKV_GROUP * RWKV_HEAD_DIM
    if s0_layer is None:
        s0_spec = pl.BlockSpec(state_block, lambda gi, ci: (gi, 0, 0, 0))
    else:
        s0_spec = pl.BlockSpec((None,) + state_block, lambda gi, ci: (s0_layer, gi, 0, 0, 0))
    return pl.pallas_call(
        kern,
        grid=(g_, nc),
        in_specs=[
            pl.BlockSpec((1, chunk, pr), lambda gi, ci: (gi, ci, 0)),
            pl.BlockSpec((1, 1, pr), lambda gi, ci: (gi, 0, 0)),
            s0_spec,
            pl.BlockSpec((1, pr), const2),
            pl.BlockSpec((SUBLANES, RWKV_WIDTH), const2),
            pl.BlockSpec((LANES, RWKV_WIDTH), const2),
            pl.BlockSpec((LANES, RWKV_WIDTH), const2),
            pl.BlockSpec((2 * LANES, RWKV_WIDTH), const2),
            pl.BlockSpec((RWKV_WIDTH, LANES), const2),
            pl.BlockSpec((LANES, RWKV_WIDTH), const2),
        ],
        out_specs=[
            pl.BlockSpec((1, chunk, RWKV_WIDTH), lambda gi, ci: (gi, ci, 0)),
            pl.BlockSpec(state_block, lambda gi, ci: (gi, 0, 0, 0)),
            pl.BlockSpec((1, 1, pr), lambda gi, ci: (gi, 0, 0)),
        ],
        out_shape=[
            jax.ShapeDtypeStruct((g_, t_, RWKV_WIDTH), BF16),
            jax.ShapeDtypeStruct((g_,) + state_block[1:], F32),
            jax.ShapeDtypeStruct((g_, 1, pr), F32),
        ],
        scratch_shapes=[pltpu.VMEM((1, pr), F32),
                        pltpu.VMEM((RWKV_HEADS // RWKV_GROUP, group_lanes, group_lanes), F32)],
        compiler_params=_cparams("parallel", "arbitrary"),
        name="rwkv7_chunk",
    )(z, shift0, s0, lw["mu"], lw["vecs"], lw["ww"], lw["wa"], lw["wg"], lw["seg"], lw["segt"])


def _rwkv_weights(l, mu_shift, w0, w_lora, a0, a_lora, g_lora, k_k, k_a, r_k, lnx_w, lnx_b):
    pad = RWKV_PROJ_PAD - RWKV_PROJ
    vecs = jnp.stack([w0[l], a0[l], k_k[l], k_a[l], r_k[l].reshape(-1), lnx_w[l], lnx_b[l],
                      jnp.zeros((RWKV_WIDTH,), F32)])
    zeros64 = jnp.zeros((DECAY_LORA, RWKV_WIDTH), F32)
    head_of_lane = np.arange(RWKV_WIDTH) // RWKV_HEAD_DIM
    seg = (head_of_lane[:, None] == np.arange(LANES)[None, :]).astype(np.float32)
    return {
        "mu": jnp.pad(mu_shift[l], (0, pad))[None, :],
        "vecs": vecs,
        "ww": jnp.concatenate([w_lora[l], zeros64]).astype(BF16),
        "wa": jnp.concatenate([zeros64, a_lora[l]]).astype(BF16),
        "wg": jnp.pad(g_lora[l], ((0, 2 * LANES - GATE_LORA), (0, 0))).astype(BF16),
        "seg": jnp.asarray(seg, BF16),
        "segt": jnp.asarray(seg.T, BF16),
    }


def _ada_kernel(c_ref, w_ref, b_ref, o_ref):
    c = c_ref[...]
    act = (c * jax.nn.sigmoid(c)).astype(BF16)
    o_ref[0] = _dot(act, w_ref[0].astype(BF16)) + b_ref[0]


def _ada_mod(c_all, w_ada, b_ada, *, tn=2048):
    r_, d = c_all.shape
    l_, _, n = w_ada.shape
    return pl.pallas_call(
        _ada_kernel,
        grid=(l_, n // tn),
        in_specs=[
            pl.BlockSpec((r_, d), lambda li, j: (0, 0)),
            pl.BlockSpec((1, d, tn), lambda li, j: (li, 0, j)),
            pl.BlockSpec((1, 1, tn), lambda li, j: (li, 0, j)),
        ],
        out_specs=pl.BlockSpec((1, r_, tn), lambda li, j: (li, 0, j)),
        out_shape=jax.ShapeDtypeStruct((l_, r_, n), F32),
        compiler_params=_cparams("parallel", "parallel"),
        name="ada_mod",
    )(c_all, w_ada, b_ada.reshape(l_, 1, n))


def _mod_spec(mod, chunk, width, tt):
    per_token = mod.shape[1] != 1
    rows = tt if per_token else 1
    blocks_per_chunk = D_MODEL // width

    def index(gi, ti, *rest):
        j = rest[0] if rest else 0
        return (gi, ti if per_token else 0, chunk * blocks_per_chunk + j)

    return pl.BlockSpec((1, rows, width), index)


def _norm_kernel(x_ref, gain_ref, sc_ref, sh_ref, o_ref):
    x = x_ref[0]
    y = x * lax.rsqrt(jnp.mean(x * x, axis=-1, keepdims=True) + RMS_EPS) * gain_ref[...]
    o_ref[0] = (y * (1.0 + sc_ref[0]) + sh_ref[0]).astype(o_ref.dtype)


def _norm_mod(x, gain, mod, sh_chunk, sc_chunk, *, tt):
    g_, t_, d = x.shape
    return pl.pallas_call(
        _norm_kernel,
        grid=(g_, t_ // tt),
        in_specs=[
            pl.BlockSpec((1, tt, d), lambda gi, ti: (gi, ti, 0)),
            pl.BlockSpec((1, d), lambda gi, ti: (0, 0)),
            _mod_spec(mod, sc_chunk, d, tt),
            _mod_spec(mod, sh_chunk, d, tt),
        ],
        out_specs=pl.BlockSpec((1, tt, d), lambda gi, ti: (gi, ti, 0)),
        out_shape=jax.ShapeDtypeStruct((g_, t_, d), BF16),
        compiler_params=_cparams("parallel", "parallel"),
        name="norm_mod",
    )(x, gain.reshape(1, d), mod, mod)


def _matmul(name, lhs, ws, pairs, epilogue, out_dtypes, *, tm, tn, tiles=(), mods=(), cols=()):
    g_, t_, _ = lhs[0].shape
    w_arrays = [w[0] if isinstance(w, tuple) else w for w in ws]
    n = w_arrays[0].shape[-1]
    nl, nw, nt, nm, nc = len(lhs), len(ws), len(tiles), len(mods), len(cols)

    def w_spec(w):
        if isinstance(w, tuple):
            layer = w[1]
            return pl.BlockSpec((None, w[0].shape[1], tn), lambda gi, ti, j: (layer, 0, j))
        return pl.BlockSpec((w.shape[0], tn), lambda gi, ti, j: (0, j))

    def body(*refs):
        lhs_refs = refs[:nl]
        w_refs = refs[nl:nl + nw]
        tile_refs = refs[nl + nw:nl + nw + nt]
        mod_refs = refs[nl + nw + nt:nl + nw + nt + nm]
        col_refs = refs[nl + nw + nt + nm:nl + nw + nt + nm + nc]
        out_refs = refs[len(refs) - len(out_dtypes):]
        dots = [_dot(lhs_refs[pi][0], w_refs[i][...].astype(BF16)) for i, pi in enumerate(pairs)]
        outs = epilogue(dots, [r[0] for r in tile_refs], [r[0] for r in mod_refs], [r[...] for r in col_refs])
        for o_ref, o in zip(out_refs, outs):
            o_ref[0] = o.astype(o_ref.dtype)

    in_specs = [pl.BlockSpec((1, tm, a.shape[2]), lambda gi, ti, j: (gi, ti, 0)) for a in lhs]
    in_specs += [w_spec(w) for w in ws]
    in_specs += [pl.BlockSpec((1, tm, tn), lambda gi, ti, j: (gi, ti, j)) for _ in tiles]
    in_specs += [_mod_spec(m, chunk, tn, tm) for m, chunk in mods]
    in_specs += [pl.BlockSpec((1, tn), lambda gi, ti, j: (0, j)) for _ in cols]
    return pl.pallas_call(
        body,
        grid=(g_, t_ // tm, n // tn),
        in_specs=in_specs,
        out_specs=[pl.BlockSpec((1, tm, tn), lambda gi, ti, j: (gi, ti, j)) for _ in out_dtypes],
        out_shape=[jax.ShapeDtypeStruct((g_, t_, n), dt) for dt in out_dtypes],
        compiler_params=_cparams("parallel", "parallel", "arbitrary"),
        name=name,
    )(*lhs, *w_arrays, *tiles, *[m for m, _ in mods], *cols)


def _ep_plain(dots, tiles, mods, cols):
    return [dots[0]]


def _ep_plain2(dots, tiles, mods, cols):
    return [dots[0], dots[0]]


def _ep_head_norm(dots, tiles, mods, cols):
    x = dots[0]
    parts = []
    for j in range(x.shape[1] // MOBA_HEAD_DIM):
        xs = x[:, j * MOBA_HEAD_DIM:(j + 1) * MOBA_HEAD_DIM]
        parts.append(xs * lax.rsqrt(jnp.mean(xs * xs, axis=-1, keepdims=True) + RMS_EPS))
    y = jnp.concatenate(parts, axis=1) * cols[0]
    return [y, y]


def _ep_merge(dots, tiles, mods, cols):
    return [jax.nn.sigmoid(dots[0]) * dots[2] + jax.nn.sigmoid(dots[1]) * dots[3]]


def _ep_residual(dots, tiles, mods, cols):
    return [tiles[0] + mods[0] * dots[0]]


def _ep_swiglu(dots, tiles, mods, cols):
    gate = dots[0]
    return [gate * jax.nn.sigmoid(gate) * dots[1]]


def _t5_bucket(dist):
    n = jnp.maximum(dist, 0)
    max_exact = NUM_BUCKETS // 2
    nf = jnp.maximum(n, 1).astype(F32)
    large = max_exact + (jnp.log(nf / max_exact) / math.log(MAX_DISTANCE / max_exact)
                         * (NUM_BUCKETS - max_exact)).astype(jnp.int32)
    large = jnp.minimum(large, NUM_BUCKETS - 1)
    return jnp.where(n < max_exact, n, large)


def _top_blocks(gates, lane_f):
    sel = jnp.zeros_like(gates)
    for _ in range(MOBA_TOPK):
        mx = jnp.max(gates, axis=-1, keepdims=True)
        first = jnp.min(jnp.where(gates == mx, lane_f, float(LANES)), axis=-1, keepdims=True)
        pick = (lane_f == first) & (mx > 0.5 * NEG_BIG)
        sel = jnp.where(pick, 1.0, sel)
        gates = jnp.where(pick, NEG_BIG, gates)
    return sel


def _kmean_kernel(k_ref, o_ref):
    n = pl.program_id(1)
    o_ref[0, pl.ds(n, 1), :] = jnp.mean(k_ref[0], axis=0, keepdims=True)


def _block_means(k):
    b_, t_, _ = k.shape
    nb = t_ // MOBA_BLOCK
    return pl.pallas_call(
        _kmean_kernel,
        grid=(b_, nb),
        in_specs=[pl.BlockSpec((1, MOBA_BLOCK, MOBA_WIDTH), lambda bi, n: (bi, n, 0))],
        out_specs=pl.BlockSpec((1, nb, MOBA_WIDTH), lambda bi, n: (bi, 0, 0)),
        out_shape=jax.ShapeDtypeStruct((b_, nb, MOBA_WIDTH), F32),
        compiler_params=_cparams("parallel", "arbitrary"),
        name="moba_block_means",
    )(k)


def _bias_tile_kernel(rt_ref, o_ref, *, n_blocks):
    delta = pl.program_id(1)
    blk = MOBA_BLOCK
    start = pl.multiple_of((n_blocks - 1 - delta) * blk, blk)
    window = rt_ref[0, :, pl.ds(start, 2 * blk)]
    rolled = pltpu.roll(jnp.broadcast_to(window, (blk, 2 * blk)), 0, 1, stride=1, stride_axis=0)
    r_i = lax.broadcasted_iota(jnp.int32, (blk, blk), 0)
    c_i = lax.broadcasted_iota(jnp.int32, (blk, blk), 1)
    on_diag = (delta == 0).astype(jnp.int32)
    o_ref[0, 0] = jnp.where((c_i - r_i) * on_diag <= 0, rolled[:, blk:2 * blk], NEG_BIG)


def _bias_tiles(rev_bias, n_blocks):
    h_ = rev_bias.shape[0]
    blk = MOBA_BLOCK
    return pl.pallas_call(
        functools.partial(_bias_tile_kernel, n_blocks=n_blocks),
        grid=(h_, n_blocks),
        in_specs=[pl.BlockSpec((1, 1, rev_bias.shape[2]), lambda h, dl: (h, 0, 0))],
        out_specs=pl.BlockSpec((1, 1, blk, blk), lambda h, dl: (h, dl, 0, 0)),
        out_shape=jax.ShapeDtypeStruct((h_, n_blocks, blk, blk), F32),
        compiler_params=_cparams("parallel", "parallel"),
        name="moba_bias_tiles",
    )(rev_bias)


def _moba_prompt_kernel(q_ref, k_ref, v_ref, km_ref, bias_ref, o_ref, qa_sc, m_sc, l_sc, acc_sc, *, n_blocks):
    i = pl.program_id(2)
    blk = MOBA_BLOCK
    nb = n_blocks
    q = q_ref[0]
    km1, km2, km3 = _split3(km_ref[0])
    q1, q2, q3 = _split3(q)
    g1 = _dot_nt(jnp.concatenate([km1, km2, km3], axis=0), q1)
    g2 = _dot_nt(jnp.concatenate([km1, km2], axis=0), q2)
    g3 = _dot_nt(km1, q3)
    gates = (g1[2 * nb:3 * nb] + g2[nb:2 * nb] + g3) + (g1[nb:2 * nb] + g2[0:nb]) + g1[0:nb]
    n_i = lax.broadcasted_iota(jnp.int32, (nb, blk), 0)
    n_f = n_i.astype(F32)
    gates = jnp.where(n_i < i, gates, NEG_BIG)
    sel = jnp.where(n_i == i, 1.0, 0.0)
    for _ in range(MOBA_TOPK):
        mx = jnp.max(gates, axis=0, keepdims=True)
        first = jnp.min(jnp.where(gates == mx, n_f, float(nb)), axis=0, keepdims=True)
        pick = (n_f == first) & (mx > 0.5 * NEG_BIG)
        sel = jnp.where(pick, 1.0, sel)
        gates = jnp.where(pick, NEG_BIG, gates)
    hidden = jnp.where(sel > 0.0, 0.0, NEG_BIG)
    hidden = jnp.concatenate([hidden, jnp.zeros((LANES - nb, blk), F32)], axis=0).T
    qa_sc[...] = jnp.concatenate([(q * (MOBA_HEAD_DIM ** -0.5)).astype(BF16), hidden.astype(BF16)], axis=1)
    m_sc[...] = jnp.full(m_sc.shape, NEG_BIG, F32)
    l_sc[...] = jnp.zeros(l_sc.shape, F32)
    acc_sc[...] = jnp.zeros(acc_sc.shape, F32)
    lane_i = lax.broadcasted_iota(jnp.int32, (blk, LANES), 1)

    def logits_of_pair(jj):
        qa = qa_sc[...]
        parts = []
        for j_raw in (2 * jj, 2 * jj + 1):
            j = jnp.minimum(j_raw, nb - 1)
            rows = pl.ds(pl.multiple_of(j * blk, blk), blk)
            k_aug = jnp.concatenate([k_ref[0, rows, :], jnp.where(lane_i == j_raw, 1.0, 0.0).astype(BF16)], axis=1)
            parts.append(_dot_nt(qa, k_aug) + bias_ref[0, jnp.clip(i - j_raw, 0, nb - 1)])
        return jnp.concatenate(parts, axis=1)

    def pair(jj, s_cur):
        s_next = logits_of_pair(jj + 1)
        vs = [v_ref[0, pl.ds(pl.multiple_of(j * blk, blk), blk), :] for j in (2 * jj, 2 * jj + 1)]
        cols = [s_cur[:, x * LANES:(x + 1) * LANES] for x in range(2 * blk // LANES)]
        top = cols[0]
        for x in cols[1:]:
            top = jnp.maximum(top, x)
        m_prev = m_sc[...]
        m_new = jnp.maximum(m_prev, jnp.max(top, axis=-1, keepdims=True))
        ps = [jnp.exp(x - m_new) for x in cols]
        tot = ps[0]
        for x in ps[1:]:
            tot = tot + x
        alpha = jnp.exp(m_prev - m_new)
        l_sc[...] = alpha * l_sc[...] + jnp.sum(tot, axis=-1, keepdims=True)
        p_all = jnp.concatenate(ps, axis=1).astype(BF16)
        acc_sc[...] = alpha * acc_sc[...] + _dot(p_all, jnp.concatenate(vs, axis=0))
        m_sc[...] = m_new
        return s_next

    lax.fori_loop(0, (i + 2) // 2, pair, logits_of_pair(0))
    o_ref[0] = (acc_sc[...] / l_sc[...]).astype(o_ref.dtype)


def _moba_prompt(q, k_bf, v_bf, kmean, bias_tiles):
    b_, t_, _ = q.shape
    nb = t_ // MOBA_BLOCK
    h_ = MOBA_HEADS
    blk = MOBA_BLOCK
    return pl.pallas_call(
        functools.partial(_moba_prompt_kernel, n_blocks=nb),
        grid=(b_, h_, nb),
        in_specs=[
            pl.BlockSpec((1, blk, LANES), lambda bi, h, i: (bi, i, h)),
            pl.BlockSpec((1, t_, LANES), lambda bi, h, i: (bi, 0, h)),
            pl.BlockSpec((1, t_, LANES), lambda bi, h, i: (bi, 0, h)),
            pl.BlockSpec((1, nb, LANES), lambda bi, h, i: (bi, 0, h)),
            pl.BlockSpec((1, nb, blk, blk), lambda bi, h, i: (h, 0, 0, 0)),
        ],
        out_specs=pl.BlockSpec((1, blk, LANES), lambda bi, h, i: (bi, i, h)),
        out_shape=jax.ShapeDtypeStruct((b_, t_, MOBA_WIDTH), BF16),
        scratch_shapes=[pltpu.VMEM((blk, 2 * LANES), BF16), pltpu.VMEM((blk, LANES), F32),
                        pltpu.VMEM((blk, LANES), F32), pltpu.VMEM((blk, LANES), F32)],
        compiler_params=_cparams("parallel", "parallel", "arbitrary"),
        name="moba_prompt",
    )(q, k_bf, v_bf, kmean, bias_tiles)


def _prompt_bias_table(rel_bias, t_):
    y = jnp.arange(t_ + MOBA_BLOCK, dtype=jnp.int32)
    tab = rel_bias[_t5_bucket(t_ - y)]
    tab = jnp.where((y <= t_)[:, None], tab, 0.0)
    return tab.T.reshape(MOBA_HEADS, 1, t_ + MOBA_BLOCK)


def _moba_sample_kernel(pt_ref, q_ref, *refs, n_past):
    n_pages = SAMPLE_BLOCKS_PER_STEP * PAGES_PER_BLOCK
    k_refs, v_refs = refs[:n_pages], refs[n_pages:2 * n_pages]
    kn_ref, vn_ref, bias_ref, ownb_ref, o_ref, m_sc, l_sc, g_sc, acc_sc = refs[2 * n_pages:]
    step = pl.program_id(1)
    scale = MOBA_HEAD_DIM ** -0.5
    rows_n = q_ref.shape[1]
    cols_n = MOBA_BLOCK * MOBA_HEADS
    q = q_ref[0]
    q_bf = (q * scale).astype(BF16)
    shape = (rows_n, LANES)
    for blk in range(SAMPLE_BLOCKS_PER_STEP):
        n = step * SAMPLE_BLOCKS_PER_STEP + blk
        pages = slice(blk * PAGES_PER_BLOCK, (blk + 1) * PAGES_PER_BLOCK)
        kf = jnp.concatenate([r_[...] for r_ in k_refs[pages]], axis=0)
        vf = jnp.concatenate([r_[...] for r_ in v_refs[pages]], axis=0)
        logits = _dot_nt(q_bf, kf.astype(BF16)) + bias_ref[:, pl.ds(pl.multiple_of(n * cols_n, cols_n), cols_n)]
        m_b = jnp.max(logits, axis=-1, keepdims=True)
        p = jnp.exp(logits - m_b)
        m_sc[n] = jnp.broadcast_to(m_b, shape)
        l_sc[n] = jnp.broadcast_to(jnp.sum(p, axis=-1, keepdims=True), shape)
        acc_sc[n] = _dot(p.astype(BF16), vf.astype(BF16))
        kmean = jnp.mean(kf.reshape(MOBA_BLOCK, MOBA_HEADS, MOBA_HEAD_DIM), axis=0)
        gate = jnp.concatenate([jnp.sum(q[t * MOBA_HEADS:(t + 1) * MOBA_HEADS] * kmean, axis=-1, keepdims=True)
                                for t in range(rows_n // MOBA_HEADS)], axis=0)
        g_sc[n] = jnp.broadcast_to(gate, shape)

    @pl.when(step == pl.num_programs(1) - 1)
    def _():
        lane_f = lax.broadcasted_iota(jnp.int32, (1, LANES), 1).astype(F32)
        gates = jnp.full(shape, NEG_BIG, F32)
        for nb in range(n_past):
            gates = jnp.where(lane_f == float(nb), g_sc[nb], gates)
        sel = _top_blocks(gates, lane_f)
        s_own = _dot_nt(q_bf, kn_ref[0].astype(BF16)) + ownb_ref[...]
        m_tot = jnp.broadcast_to(jnp.max(s_own, axis=-1, keepdims=True), shape)
        picked = []
        for nb in range(n_past):
            on = jnp.max(jnp.where(lane_f == float(nb), sel, 0.0), axis=-1, keepdims=True) > 0.0
            picked.append(on)
            m_tot = jnp.maximum(m_tot, jnp.where(on, m_sc[nb], NEG_BIG))
        p_own = jnp.exp(s_own - m_tot[:, 0:rows_n])
        l_tot = jnp.broadcast_to(jnp.sum(p_own, axis=-1, keepdims=True), shape)
        acc = _dot(p_own.astype(BF16), vn_ref[0].astype(BF16))
        for nb in range(n_past):
            w = jnp.where(picked[nb], jnp.exp(m_sc[nb] - m_tot), 0.0)
            l_tot = l_tot + w * l_sc[nb]
            acc = acc + w * acc_sc[nb]
        o_ref[0] = (acc / l_tot).astype(o_ref.dtype)


def _moba_sample(layer, page_table, q, k_new, v_new, cache_k, cache_v, bias_tab, own_bias):
    s_, rows_n, _ = q.shape
    n_past = page_table.shape[1] * PAGE_SIZE // MOBA_BLOCK
    row = pl.BlockSpec((1, rows_n, MOBA_HEAD_DIM), lambda si, n, pt: (si, 0, 0))

    pages_per_step = SAMPLE_BLOCKS_PER_STEP * PAGES_PER_BLOCK

    def page_spec(pg):
        return pl.BlockSpec((None, None, PAGE_SIZE * MOBA_HEADS, MOBA_HEAD_DIM),
                            lambda si, n, pt: (layer, pt[si, pages_per_step * n + pg], 0, 0))

    tiles = [page_spec(pg) for pg in range(pages_per_step)]
    rows_shape = cache_k.shape[:2] + (PAGE_SIZE * MOBA_HEADS, MOBA_HEAD_DIM)
    cache_k = cache_k.reshape(rows_shape)
    cache_v = cache_v.reshape(rows_shape)
    grid_spec = pltpu.PrefetchScalarGridSpec(
        num_scalar_prefetch=1,
        grid=(s_, n_past // SAMPLE_BLOCKS_PER_STEP),
        in_specs=[row] + tiles + tiles + [
            row,
            row,
            pl.BlockSpec(bias_tab.shape, lambda si, n, pt: (0, 0)),
            pl.BlockSpec(own_bias.shape, lambda si, n, pt: (0, 0)),
        ],
        out_specs=row,
        scratch_shapes=[pltpu.VMEM((n_past, rows_n, LANES), F32) for _ in range(4)],
    )
    return pl.pallas_call(
        functools.partial(_moba_sample_kernel, n_past=n_past),
        grid_spec=grid_spec,
        out_shape=jax.ShapeDtypeStruct((s_, rows_n, MOBA_HEAD_DIM), BF16),
        compiler_params=_cparams("parallel", "arbitrary"),
        name="moba_sample",
    )(page_table, q, *([cache_k] * len(tiles)), *([cache_v] * len(tiles)), k_new, v_new, bias_tab, own_bias)


def _sample_bias_tables(rel_bias, past_len, t_new):
    h_ = rel_bias.shape[1]
    same_head = jnp.eye(h_, dtype=bool)
    dist = past_len + t_new - 1 - jnp.arange(past_len + t_new, dtype=jnp.int32)
    by_dist_rev = rel_bias[_t5_bucket(dist)]
    tab = jnp.stack([by_dist_rev[t_new - 1 - t:t_new - 1 - t + past_len] for t in range(t_new)])
    per_key = jnp.transpose(tab, (0, 2, 1)).reshape(t_new * h_, past_len)
    t = jnp.arange(t_new, dtype=jnp.int32)
    own = rel_bias[_t5_bucket(t[:, None] - t[None, :])]
    own = jnp.where((t[None, :] <= t[:, None])[..., None], own, NEG_BIG)
    own = jnp.where(same_head[None, :, None, :], jnp.transpose(own, (0, 2, 1))[..., None], NEG_BIG)
    return _spread_over_heads(per_key, h_), own.reshape(t_new * h_, t_new * h_)


def _spread_kernel(x_ref, e_ref, neg_ref, o_ref):
    hi, mid, lo = _split3(x_ref[...])
    e = e_ref[...]
    o_ref[...] = (_dot(hi, e) + _dot(mid, e) + _dot(lo, e)) + neg_ref[...]


def _spread_over_heads(per_key, h_):
    rows_n, n_keys = per_key.shape
    blk = MOBA_BLOCK
    spread = np.zeros((blk, blk * h_), np.float32)
    spread[np.arange(blk * h_) // h_, np.arange(blk * h_)] = 1.0
    other_head = np.where((np.arange(rows_n) % h_)[:, None] == (np.arange(blk * h_) % h_)[None, :], 0.0, NEG_BIG)
    return pl.pallas_call(
        _spread_kernel,
        grid=(n_keys // blk,),
        in_specs=[
            pl.BlockSpec((rows_n, blk), lambda n: (0, n)),
            pl.BlockSpec(spread.shape, lambda n: (0, 0)),
            pl.BlockSpec(other_head.shape, lambda n: (0, 0)),
        ],
        out_specs=pl.BlockSpec((rows_n, blk * h_), lambda n: (0, n)),
        out_shape=jax.ShapeDtypeStruct((rows_n, n_keys * h_), F32),
        compiler_params=_cparams("parallel"),
        name="moba_sample_bias",
    )(per_key, jnp.asarray(spread, BF16), jnp.asarray(other_head, F32))


PROMPT_ROW_TILE = 1024

SH1, SC1, G1, SH2, SC2, G2 = range(6)


def _layer_weights(l, w_in, q_norm, k_norm, w_up_a, w_up_b, w_out, w_ffn_gate, w_ffn_up, w_ffn_down):
    o_q = RWKV_PROJ
    o_v = o_q + 2 * MOBA_WIDTH
    o_ga = o_v + MOBA_WIDTH
    o_gb = o_ga + D_MODEL
    wl = w_in[l]
    heads = MOBA_WIDTH // MOBA_HEAD_DIM
    return {
        "rwkv": jnp.pad(wl[:, :RWKV_PROJ], ((0, 0), (0, RWKV_PROJ_PAD - RWKV_PROJ))).astype(BF16),
        "q": wl[:, o_q:o_q + MOBA_WIDTH].astype(BF16),
        "k": wl[:, o_q + MOBA_WIDTH:o_v].astype(BF16),
        "v": wl[:, o_v:o_ga].astype(BF16),
        "ga": wl[:, o_ga:o_gb].astype(BF16),
        "gb": wl[:, o_gb:].astype(BF16),
        "q_gain": jnp.tile(q_norm[l], heads)[None, :],
        "k_gain": jnp.tile(k_norm[l], heads)[None, :],
        "up_a": (w_up_a, l),
        "up_b": (w_up_b, l),
        "out": w_out[l].astype(BF16),
        "fg": (w_ffn_gate, l),
        "fu": (w_ffn_up, l),
        "fd": (w_ffn_down, l),
    }


def _trunk_layer(x, mod, norm_mix, norm_ffn, w, rwkv_fn, attn_fn, *, tm):
    h = _norm_mod(x, norm_mix, mod, SH1, SC1, tt=min(tm, 512))
    z_r = _matmul("in_rwkv", [h], [w["rwkv"]], [0], _ep_plain, [F32], tm=tm, tn=RWKV_PROJ_PAD // 3)[0]
    q = _matmul("in_q", [h], [w["q"]], [0], _ep_head_norm, [F32], tm=tm, tn=1024, cols=[w["q_gain"]])[0]
    k, k_bf = _matmul("in_k", [h], [w["k"]], [0], _ep_head_norm, [F32, BF16], tm=tm, tn=1024, cols=[w["k_gain"]])
    v, v_bf = _matmul("in_v", [h], [w["v"]], [0], _ep_plain2, [F32, BF16], tm=tm, tn=1024)
    y_a, s_new, shift_new = rwkv_fn(z_r)
    y_b = attn_fn(q, k, k_bf, v, v_bf)
    merged = _matmul("merge", [h, y_a, y_b], [w["ga"], w["gb"], w["up_a"], w["up_b"]], [0, 0, 1, 2],
                     _ep_merge, [BF16], tm=tm, tn=512)[0]
    x1 = _matmul("out_proj", [merged], [w["out"]], [0], _ep_residual, [F32], tm=min(2 * tm, x.shape[1]), tn=512,
                 tiles=[x], mods=[(mod, G1)])[0]
    h2 = _norm_mod(x1, norm_ffn, mod, SH2, SC2, tt=min(tm, 512))
    act = _matmul("ffn_in", [h2], [w["fg"], w["fu"]], [0, 0], _ep_swiglu, [BF16], tm=tm, tn=512)[0]
    x2 = _matmul("ffn_out", [act], [w["fd"]], [0], _ep_residual, [F32], tm=tm, tn=256,
                 tiles=[x1], mods=[(mod, G2)])[0]
    return x2, k, v, s_new, shift_new


def kernel(x_prompt, x_sample, cache_k, cache_v, state_wkv, state_shift, page_table, c_prompt, c_sample, rel_bias, w_ada, b_ada, norm_mix, norm_ffn, w_in, mu_shift, w0, w_lora, a0, a_lora, g_lora, k_k, k_a, r_k, lnx_w, lnx_b, q_norm, k_norm, w_up_a, w_up_b, w_out, w_ffn_gate, w_ffn_up, w_ffn_down):
    bp, tp, d = x_prompt.shape
    bs, ts, _ = x_sample.shape
    n_seq = bp + bs
    c_all = jnp.concatenate([c_prompt, c_sample])
    c_all = jnp.pad(c_all, ((0, -n_seq % SUBLANES), (0, 0)))
    mod = _ada_mod(c_all, w_ada, b_ada)

    past_len = page_table.shape[1] * PAGE_SIZE
    bias_tiles = _bias_tiles(_prompt_bias_table(rel_bias, tp), tp // MOBA_BLOCK)
    bias_tab, own_bias = _sample_bias_tables(rel_bias, past_len, ts)
    pr_pad = RWKV_PROJ_PAD - RWKV_PROJ
    sample_chunk = SUBLANES

    def sample_rows(a):
        return a.reshape(bs, ts * MOBA_HEADS, MOBA_HEAD_DIM)

    x_p = x_prompt
    x_s = x_sample.reshape(1, bs * ts, d)
    outs = [[] for _ in range(8)]
    for l in range(DEPTH):
        w = _layer_weights(l, w_in, q_norm, k_norm, w_up_a, w_up_b, w_out, w_ffn_gate, w_ffn_up, w_ffn_down)
        lw = _rwkv_weights(l, mu_shift, w0, w_lora, a0, a_lora, g_lora, k_k, k_a, r_k, lnx_w, lnx_b)
        mod_p = mod[l, :bp][:, None, :]
        mod_s = jnp.repeat(mod[l, bp:n_seq], ts, axis=0)[None]

        def rwkv_p(z_r):
            shift0 = jnp.zeros((bp, 1, RWKV_PROJ_PAD), F32)
            s0 = jnp.zeros((bp, RWKV_HEADS, RWKV_HEAD_DIM, RWKV_HEAD_DIM), F32)
            return _rwkv_branch(z_r, shift0, s0, lw, chunk=64, t_valid=tp)

        def attn_p(q, k, k_bf, v, v_bf):
            return _moba_prompt(q, k_bf, v_bf, _block_means(k), bias_tiles)

        x_p, k_p, v_p, s_p, sh_p = _trunk_layer(x_p, mod_p, norm_mix[l], norm_ffn[l], w, rwkv_p, attn_p,
                                                 tm=min(tp, PROMPT_ROW_TILE))

        def rwkv_s(z_r):
            z = z_r.reshape(bs, ts, RWKV_PROJ_PAD)
            z = jnp.pad(z, ((0, 0), (0, sample_chunk - ts), (0, 0)))
            shift0 = jnp.pad(state_shift[l], ((0, 0), (0, pr_pad)))[:, None, :]
            y, s_new, shift_new = _rwkv_branch(z, shift0, state_wkv, lw, chunk=sample_chunk, t_valid=ts,
                                               s0_layer=l)
            return y[:, :ts].reshape(1, bs * ts, RWKV_WIDTH), s_new, shift_new

        def attn_s(q, k, k_bf, v, v_bf):
            y = _moba_sample(l, page_table, sample_rows(q), sample_rows(k), sample_rows(v), cache_k, cache_v,
                             bias_tab, own_bias)
            return y.reshape(1, bs * ts, MOBA_WIDTH)

        x_s, k_s, v_s, s_s, sh_s = _trunk_layer(x_s, mod_s, norm_mix[l], norm_ffn[l], w, rwkv_s, attn_s,
                                                 tm=bs * ts)

        kv_shape_p = (bp, tp, MOBA_HEADS, MOBA_HEAD_DIM)
        kv_shape_s = (bs, ts, MOBA_HEADS, MOBA_HEAD_DIM)
        for lst, val in zip(outs, [
                k_p.reshape(kv_shape_p), v_p.reshape(kv_shape_p),
                s_p, sh_p[:, 0, :RWKV_PROJ],
                k_s.reshape(kv_shape_s), v_s.reshape(kv_shape_s),
                s_s, sh_s[:, 0, :RWKV_PROJ]]):
            lst.append(val)
    return (x_p, x_s.reshape(bs, ts, d)) + tuple(jnp.stack(o) for o in outs)
```

```python
import functools
import math

import jax
import jax.numpy as jnp
import numpy as np
from jax import lax
from jax.experimental import pallas as pl
from jax.experimental.pallas import tpu as pltpu

F32 = jnp.float32
BF16 = jnp.bfloat16

LANES = 128
SUBLANES = 8
VMEM_LIMIT_BYTES = 56 * 1024 * 1024

D_MODEL = 2048
DEPTH = 2
PAGE_SIZE = 128
RWKV_WIDTH = 1024
RWKV_HEAD_DIM = 64
RWKV_HEADS = 16
RWKV_GROUP = 2
DECAY_LORA = 64
ICLR_LORA = 64
GATE_LORA = 160
RWKV_PROJ = 3 * RWKV_WIDTH + DECAY_LORA + ICLR_LORA + GATE_LORA
RWKV_PROJ_PAD = 3456
LORA_OFF = 3 * RWKV_WIDTH
GN_EPS = 64e-5
MOBA_WIDTH = 1024
MOBA_HEAD_DIM = 128
MOBA_HEADS = 8
MOBA_BLOCK = 256
PAGES_PER_BLOCK = MOBA_BLOCK // PAGE_SIZE
SAMPLE_BLOCKS_PER_STEP = 8
MOBA_TOPK = 3
NUM_BUCKETS = 32
MAX_DISTANCE = 2048
D_FF = 5632
RMS_EPS = 1e-6
NEG_BIG = -1e30


def _cparams(*sem):
    return pltpu.CompilerParams(dimension_semantics=sem, vmem_limit_bytes=VMEM_LIMIT_BYTES)


def _dot(a, b):
    return jnp.dot(a, b, preferred_element_type=F32)


def _dot_nt(a, b):
    return lax.dot_general(a, b, (((1,), (1,)), ((), ())), preferred_element_type=F32)


def _dot_tn(a, b):
    return lax.dot_general(a, b, (((0,), (0,)), ((), ())), preferred_element_type=F32)


def _split2(x):
    hi = x.astype(BF16)
    lo = (x - hi.astype(F32)).astype(BF16)
    return hi, lo


def _split3(x):
    hi = x.astype(BF16)
    r1 = x - hi.astype(F32)
    mid = r1.astype(BF16)
    lo = (r1 - mid.astype(F32)).astype(BF16)
    return hi, mid, lo


def _rwkv_kernel(z_ref, shift0_ref, s0_ref, mu_ref, vecs_ref, ww_ref, wa_ref, wg_ref, seg_ref, segt_ref,
                 y_ref, sfin_ref, shift_ref, carry_z, state, *, chunk, t_valid):
    c = chunk
    ci = pl.program_id(1)
    n_chunks = pl.num_programs(1)

    hd = RWKV_HEAD_DIM

    def head_block(h):
        grp, pos = divmod(h, RWKV_GROUP)
        return grp, slice(pos * hd, (pos + 1) * hd)

    @pl.when(ci == 0)
    def _():
        carry_z[...] = shift0_ref[0]
        state[...] = jnp.zeros(state.shape, F32)
        for h in range(RWKV_HEADS):
            grp, sl = head_block(h)
            state[grp, sl, sl] = s0_ref[0, h]

    z = z_ref[0]
    row = lax.broadcasted_iota(jnp.int32, (c, 1), 0)
    z_prev = jnp.where(row == 0, carry_z[...], pltpu.roll(z, 1, axis=0))
    carry_z[...] = z[c - 1:c, :]
    last_row = (t_valid - 1) % c

    @pl.when(ci == n_chunks - 1)
    def _():
        shift_ref[0] = z[last_row:last_row + 1, :]

    zm = z + mu_ref[...] * (z_prev - z)
    r = zm[:, 0:RWKV_WIDTH]
    k = zm[:, RWKV_WIDTH:2 * RWKV_WIDTH]
    v = zm[:, 2 * RWKV_WIDTH:3 * RWKV_WIDTH]
    l0 = zm[:, LORA_OFF:LORA_OFF + LANES]
    l1 = zm[:, LORA_OFF + LANES:RWKV_PROJ_PAD]

    w0 = vecs_ref[0:1, :]
    a0 = vecs_ref[1:2, :]
    k_k = vecs_ref[2:3, :]
    k_a = vecs_ref[3:4, :]
    r_k = vecs_ref[4:5, :]
    lnx_w = vecs_ref[5:6, :]
    lnx_b = vecs_ref[6:7, :]

    wl = w0 + _dot(jnp.tanh(l0).astype(BF16), ww_ref[...])
    neg = -wl
    softplus = jnp.maximum(neg, 0.0) + jnp.log(1.0 + jnp.exp(-jnp.abs(neg)))
    logw = -jnp.exp(-softplus - 0.5)
    a = jax.nn.sigmoid(a0 + _dot(l0.astype(BF16), wa_ref[...]))
    g = _dot(jax.nn.sigmoid(l1).astype(BF16), wg_ref[...])

    seg = seg_ref[...]
    segt = segt_ref[...]

    def head_sum(x):
        hi, lo = _split2(x)
        s = _dot(hi, seg) + _dot(lo, seg)
        shi, slo = _split2(s)
        return _dot(shi, segt) + _dot(slo, segt)

    kk = k * k_k
    kk = kk / jnp.maximum(jnp.sqrt(head_sum(kk * kk)), 1e-12)
    k_h = k * (1.0 + (a - 1.0) * k_a)
    if t_valid % c != 0:
        live = (ci * c + row) < t_valid
        logw = jnp.where(live, logw, 0.0)
        kk = jnp.where(live, kk, 0.0)
        k_h = jnp.where(live, k_h, 0.0)
    b = kk * a

    ri = lax.broadcasted_iota(jnp.int32, (c, c), 0)
    cj = lax.broadcasted_iota(jnp.int32, (c, c), 1)
    tri = jnp.where(ri >= cj, 1.0, 0.0).astype(BF16)
    h1, h2, h3 = _split3(logw)
    lc = _dot(tri, h1) + _dot(tri, h2) + _dot(tri, h3)
    lc_end = lc[c - 1:c, :]
    inv_g = jnp.exp(-lc)
    r_hat = r * jnp.exp(lc)
    kap_hat = kk * jnp.exp(lc - logw)
    k_hat = k_h * inv_g
    b_hat = b * inv_g
    to_end = jnp.exp(lc_end - lc)
    k_bar = k_h * to_end
    b_bar = b * to_end
    g_end = jnp.exp(lc_end)

    gw = RWKV_GROUP * hd
    cg = RWKV_GROUP * c
    head_of_lane = lax.broadcasted_iota(jnp.int32, (1, gw), 1) // hd

    def stack(x):
        return jnp.concatenate([jnp.where(head_of_lane == j, x, 0.0) for j in range(RWKV_GROUP)],
                               axis=0).astype(BF16)

    ig = lax.broadcasted_iota(jnp.int32, (cg, cg), 0)
    jg = lax.broadcasted_iota(jnp.int32, (cg, cg), 1)
    strict = ig > jg
    lower = ig >= jg
    eye = jnp.where(ig == jg, 1.0, 0.0)

    def level(s):
        return (((ig ^ jg) & (-2 * s)) == 0) & ((ig & s) != 0) & ((jg & s) == 0)

    fused = cg % LANES == 0
    groups = range(RWKV_HEADS // RWKV_GROUP)
    lanes = [slice(grp * gw, (grp + 1) * gw) for grp in groups]
    s_old = [state[grp] for grp in groups]
    s_bf = [s.astype(BF16) for s in s_old]
    lhs = [jnp.concatenate([stack(kap_hat[:, sl]), stack(r_hat[:, sl])], axis=0) for sl in lanes]
    kh_s = [stack(k_hat[:, sl]) for sl in lanes]
    bh_s = [stack(b_hat[:, sl]) for sl in lanes]
    v_s = [stack(v[:, sl]) for sl in lanes]
    nn, mm, ark, arb = [], [], [], []
    for grp in groups:
        if fused:
            sc = _dot_nt(lhs[grp], jnp.concatenate([bh_s[grp], kh_s[grp]], axis=0))
            nn_raw, mm_raw = sc[0:cg, 0:cg], sc[0:cg, cg:2 * cg]
            arb_raw, ark_raw = sc[cg:2 * cg, 0:cg], sc[cg:2 * cg, cg:2 * cg]
        else:
            kap_s, r_s = lhs[grp][0:cg], lhs[grp][cg:2 * cg]
            nn_raw, mm_raw = _dot_nt(kap_s, bh_s[grp]), _dot_nt(kap_s, kh_s[grp])
            arb_raw, ark_raw = _dot_nt(r_s, bh_s[grp]), _dot_nt(r_s, kh_s[grp])
        nn.append(jnp.where(strict, nn_raw, 0.0))
        mm.append(jnp.where(strict, mm_raw, 0.0))
        ark.append(jnp.where(lower, ark_raw, 0.0))
        arb.append(jnp.where(lower, arb_raw, 0.0))
    from_state = [_dot_nt(lhs[grp], s_bf[grp]) for grp in groups]
    from_v = [_dot(jnp.concatenate([mm[grp], ark[grp]], axis=0).astype(BF16), v_s[grp]) for grp in groups]
    rhs = [-(from_state[grp][0:cg] + from_v[grp][0:cg]) for grp in groups]
    t_inv = [eye - jnp.where(level(1), nn[grp], 0.0) for grp in groups]
    s = 2
    while s < c:
        t_bf = [t.astype(BF16) for t in t_inv]
        half = [_dot(t_bf[grp], jnp.where(level(s), nn[grp], 0.0).astype(BF16)).astype(BF16) for grp in groups]
        t_inv = [t_inv[grp] - _dot(half[grp], t_bf[grp]) for grp in groups]
        s *= 2
    e_s = [_dot(t_inv[grp].astype(BF16), rhs[grp].astype(BF16)).astype(BF16) for grp in groups]
    y_s = [from_state[grp][cg:2 * cg] + from_v[grp][cg:2 * cg] + _dot(arb[grp].astype(BF16), e_s[grp])
           for grp in groups]
    ys = []
    for grp in groups:
        y_g = y_s[grp][0:c]
        for j in range(1, RWKV_GROUP):
            y_g = y_g + y_s[grp][j * c:(j + 1) * c]
        ys.append(y_g)
    y = jnp.concatenate(ys, axis=1)
    s_new = [s_old[grp] * g_end[:, lanes[grp]]
             + _dot_tn(jnp.concatenate([v_s[grp], e_s[grp]], axis=0),
                       jnp.concatenate([stack(k_bar[:, lanes[grp]]), stack(b_bar[:, lanes[grp]])], axis=0))
             for grp in groups]
    for grp in groups:
        state[grp] = s_new[grp]

    inv_n = 1.0 / RWKV_HEAD_DIM
    mean = head_sum(y) * inv_n
    yc = y - mean
    var = head_sum(yc * yc) * inv_n
    yn = yc * lax.rsqrt(var + GN_EPS) * lnx_w + lnx_b
    bonus = head_sum(r * k_h * r_k) * v
    y_ref[0] = ((yn + bonus) * g).astype(y_ref.dtype)

    @pl.when(ci == n_chunks - 1)
    def _():
        for h in range(RWKV_HEADS):
            grp, sl = head_block(h)
            sfin_ref[0, h] = state[grp, sl, sl]


def _rwkv_branch(z, shift0, s0, lw, *, chunk, t_valid, s0_layer=None):
    g_, t_, pr = z.shape
    nc = t_ // chunk
    kern = functools.partial(_rwkv_kernel, chunk=chunk, t_valid=t_valid)
    const2 = lambda gi, ci: (0, 0)
    state_block = (1, RWKV_HEADS, RWKV_HEAD_DIM, RWKV_HEAD_DIM)
    group_lanes = RWKV_GROUP * RWKV_HEAD_DIM
    if s0_layer is None:
        s0_spec = pl.BlockSpec(state_block, lambda gi, ci: (gi, 0, 0, 0))
    else:
        s0_spec = pl.BlockSpec((None,) + state_block, lambda gi, ci: (s0_layer, gi, 0, 0, 0))
    return pl.pallas_call(
        kern,
        grid=(g_, nc),
        in_specs=[
            pl.BlockSpec((1, chunk, pr), lambda gi, ci: (gi, ci, 0)),
            pl.BlockSpec((1, 1, pr), lambda gi, ci: (gi, 0, 0)),
            s0_spec,
            pl.BlockSpec((1, pr), const2),
            pl.BlockSpec((SUBLANES, RWKV_WIDTH), const2),
            pl.BlockSpec((LANES, RWKV_WIDTH), const2),
            pl.BlockSpec((LANES, RWKV_WIDTH), const2),
            pl.BlockSpec((2 * LANES, RWKV_WIDTH), const2),
            pl.BlockSpec((RWKV_WIDTH, LANES), const2),
            pl.BlockSpec((LANES, RWKV_WIDTH), const2),
        ],
        out_specs=[
            pl.BlockSpec((1, chunk, RWKV_WIDTH), lambda gi, ci: (gi, ci, 0)),
            pl.BlockSpec(state_block, lambda gi, ci: (gi, 0, 0, 0)),
            pl.BlockSpec((1, 1, pr), lambda gi, ci: (gi, 0, 0)),
        ],
        out_shape=[
            jax.ShapeDtypeStruct((g_, t_, RWKV_WIDTH), BF16),
            jax.ShapeDtypeStruct((g_,) + state_block[1:], F32),
            jax.ShapeDtypeStruct((g_, 1, pr), F32),
        ],
        scratch_shapes=[pltpu.VMEM((1, pr), F32),
                        pltpu.VMEM((RWKV_HEADS // RWKV_GROUP, group_lanes, group_lanes), F32)],
        compiler_params=_cparams("parallel", "arbitrary"),
        name="rwkv7_chunk",
    )(z, shift0, s0, lw["mu"], lw["vecs"], lw["ww"], lw["wa"], lw["wg"], lw["seg"], lw["segt"])


def _rwkv_weights(l, mu_shift, w0, w_lora, a0, a_lora, g_lora, k_k, k_a, r_k, lnx_w, lnx_b):
    pad = RWKV_PROJ_PAD - RWKV_PROJ
    vecs = jnp.stack([w0[l], a0[l], k_k[l], k_a[l], r_k[l].reshape(-1), lnx_w[l], lnx_b[l],
                      jnp.zeros((RWKV_WIDTH,), F32)])
    zeros64 = jnp.zeros((DECAY_LORA, RWKV_WIDTH), F32)
    head_of_lane = np.arange(RWKV_WIDTH) // RWKV_HEAD_DIM
    seg = (head_of_lane[:, None] == np.arange(LANES)[None, :]).astype(np.float32)
    return {
        "mu": jnp.pad(mu_shift[l], (0, pad))[None, :],
        "vecs": vecs,
        "ww": jnp.concatenate([w_lora[l], zeros64]).astype(BF16),
        "wa": jnp.concatenate([zeros64, a_lora[l]]).astype(BF16),
        "wg": jnp.pad(g_lora[l], ((0, 2 * LANES - GATE_LORA), (0, 0))).astype(BF16),
        "seg": jnp.asarray(seg, BF16),
        "segt": jnp.asarray(seg.T, BF16),
    }


def _ada_kernel(c_ref, w_ref, b_ref, o_ref):
    c = c_ref[...]
    act = (c * jax.nn.sigmoid(c)).astype(BF16)
    o_ref[0] = _dot(act, w_ref[0].astype(BF16)) + b_ref[0]


def _ada_mod(c_all, w_ada, b_ada, *, tn=2048):
    r_, d = c_all.shape
    l_, _, n = w_ada.shape
    return pl.pallas_call(
        _ada_kernel,
        grid=(l_, n // tn),
        in_specs=[
            pl.BlockSpec((r_, d), lambda li, j: (0, 0)),
            pl.BlockSpec((1, d, tn), lambda li, j: (li, 0, j)),
            pl.BlockSpec((1, 1, tn), lambda li, j: (li, 0, j)),
        ],
        out_specs=pl.BlockSpec((1, r_, tn), lambda li, j: (li, 0, j)),
        out_shape=jax.ShapeDtypeStruct((l_, r_, n), F32),
        compiler_params=_cparams("parallel", "parallel"),
        name="ada_mod",
    )(c_all, w_ada, b_ada.reshape(l_, 1, n))


def _mod_spec(mod, chunk, width, tt):
    per_token = mod.shape[1] != 1
    rows = tt if per_token else 1
    blocks_per_chunk = D_MODEL // width

    def index(gi, ti, *rest):
        j = rest[0] if rest else 0
        return (gi, ti if per_token else 0, chunk * blocks_per_chunk + j)

    return pl.BlockSpec((1, rows, width), index)


def _norm_kernel(x_ref, gain_ref, sc_ref, sh_ref, o_ref):
    x = x_ref[0]
    y = x * lax.rsqrt(jnp.mean(x * x, axis=-1, keepdims=True) + RMS_EPS) * gain_ref[...]
    o_ref[0] = (y * (1.0 + sc_ref[0]) + sh_ref[0]).astype(o_ref.dtype)


def _norm_mod(x, gain, mod, sh_chunk, sc_chunk, *, tt):
    g_, t_, d = x.shape
    return pl.pallas_call(
        _norm_kernel,
        grid=(g_, t_ // tt),
        in_specs=[
            pl.BlockSpec((1, tt, d), lambda gi, ti: (gi, ti, 0)),
            pl.BlockSpec((1, d), lambda gi, ti: (0, 0)),
            _mod_spec(mod, sc_chunk, d, tt),
            _mod_spec(mod, sh_chunk, d, tt),
        ],
        out_specs=pl.BlockSpec((1, tt, d), lambda gi, ti: (gi, ti, 0)),
        out_shape=jax.ShapeDtypeStruct((g_, t_, d), BF16),
        compiler_params=_cparams("parallel", "parallel"),
        name="norm_mod",
    )(x, gain.reshape(1, d), mod, mod)


def _matmul(name, lhs, ws, pairs, epilogue, out_dtypes, *, tm, tn, tiles=(), mods=(), cols=()):
    g_, t_, _ = lhs[0].shape
    w_arrays = [w[0] if isinstance(w, tuple) else w for w in ws]
    n = w_arrays[0].shape[-1]
    nl, nw, nt, nm, nc = len(lhs), len(ws), len(tiles), len(mods), len(cols)

    def w_spec(w):
        if isinstance(w, tuple):
            layer = w[1]
            return pl.BlockSpec((None, w[0].shape[1], tn), lambda gi, ti, j: (layer, 0, j))
        return pl.BlockSpec((w.shape[0], tn), lambda gi, ti, j: (0, j))

    def body(*refs):
        lhs_refs = refs[:nl]
        w_refs = refs[nl:nl + nw]
        tile_refs = refs[nl + nw:nl + nw + nt]
        mod_refs = refs[nl + nw + nt:nl + nw + nt + nm]
        col_refs = refs[nl + nw + nt + nm:nl + nw + nt + nm + nc]
        out_refs = refs[len(refs) - len(out_dtypes):]
        dots = [_dot(lhs_refs[pi][0], w_refs[i][...].astype(BF16)) for i, pi in enumerate(pairs)]
        outs = epilogue(dots, [r[0] for r in tile_refs], [r[0] for r in mod_refs], [r[...] for r in col_refs])
        for o_ref, o in zip(out_refs, outs):
            o_ref[0] = o.astype(o_ref.dtype)

    in_specs = [pl.BlockSpec((1, tm, a.shape[2]), lambda gi, ti, j: (gi, ti, 0)) for a in lhs]
    in_specs += [w_spec(w) for w in ws]
    in_specs += [pl.BlockSpec((1, tm, tn), lambda gi, ti, j: (gi, ti, j)) for _ in tiles]
    in_specs += [_mod_spec(m, chunk, tn, tm) for m, chunk in mods]
    in_specs += [pl.BlockSpec((1, tn), lambda gi, ti, j: (0, j)) for _ in cols]
    return pl.pallas_call(
        body,
        grid=(g_, t_ // tm, n // tn),
        in_specs=in_specs,
        out_specs=[pl.BlockSpec((1, tm, tn), lambda gi, ti, j: (gi, ti, j)) for _ in out_dtypes],
        out_shape=[jax.ShapeDtypeStruct((g_, t_, n), dt) for dt in out_dtypes],
        compiler_params=_cparams("parallel", "parallel", "arbitrary"),
        name=name,
    )(*lhs, *w_arrays, *tiles, *[m for m, _ in mods], *cols)


def _ep_plain(dots, tiles, mods, cols):
    return [dots[0]]


def _ep_plain2(dots, tiles, mods, cols):
    return [dots[0], dots[0]]


def _ep_head_norm(dots, tiles, mods, cols):
    x = dots[0]
    parts = []
    for j in range(x.shape[1] // MOBA_HEAD_DIM):
        xs = x[:, j * MOBA_HEAD_DIM:(j + 1) * MOBA_HEAD_DIM]
        parts.append(xs * lax.rsqrt(jnp.mean(xs * xs, axis=-1, keepdims=True) + RMS_EPS))
    y = jnp.concatenate(parts, axis=1) * cols[0]
    return [y, y]


def _ep_merge(dots, tiles, mods, cols):
    return [jax.nn.sigmoid(dots[0]) * dots[2] + jax.nn.sigmoid(dots[1]) * dots[3]]


def _ep_residual(dots, tiles, mods, cols):
    return [tiles[0] + mods[0] * dots[0]]


def _ep_swiglu(dots, tiles, mods, cols):
    gate = dots[0]
    return [gate * jax.nn.sigmoid(gate) * dots[1]]


def _t5_bucket(dist):
    n = jnp.maximum(dist, 0)
    max_exact = NUM_BUCKETS // 2
    nf = jnp.maximum(n, 1).astype(F32)
    large = max_exact + (jnp.log(nf / max_exact) / math.log(MAX_DISTANCE / max_exact)
                         * (NUM_BUCKETS - max_exact)).astype(jnp.int32)
    large = jnp.minimum(large, NUM_BUCKETS - 1)
    return jnp.where(n < max_exact, n, large)


def _top_blocks(gates, lane_f):
    sel = jnp.zeros_like(gates)
    for _ in range(MOBA_TOPK):
        mx = jnp.max(gates, axis=-1, keepdims=True)
        first = jnp.min(jnp.where(gates == mx, lane_f, float(LANES)), axis=-1, keepdims=True)
        pick = (lane_f == first) & (mx > 0.5 * NEG_BIG)
        sel = jnp.where(pick, 1.0, sel)
        gates = jnp.where(pick, NEG_BIG, gates)
    return sel


def _kmean_kernel(k_ref, o_ref):
    n = pl.program_id(1)
    o_ref[0, pl.ds(n, 1), :] = jnp.mean(k_ref[0], axis=0, keepdims=True)


def _block_means(k):
    b_, t_, _ = k.shape
    nb = t_ // MOBA_BLOCK
    return pl.pallas_call(
        _kmean_kernel,
        grid=(b_, nb),
        in_specs=[pl.BlockSpec((1, MOBA_BLOCK, MOBA_WIDTH), lambda bi, n: (bi, n, 0))],
        out_specs=pl.BlockSpec((1, nb, MOBA_WIDTH), lambda bi, n: (bi, 0, 0)),
        out_shape=jax.ShapeDtypeStruct((b_, nb, MOBA_WIDTH), F32),
        compiler_params=_cparams("parallel", "arbitrary"),
        name="moba_block_means",
    )(k)


def _bias_tile_kernel(rt_ref, o_ref, *, n_blocks):
    delta = pl.program_id(1)
    blk = MOBA_BLOCK
    start = pl.multiple_of((n_blocks - 1 - delta) * blk, blk)
    window = rt_ref[0, :, pl.ds(start, 2 * blk)]
    rolled = pltpu.roll(jnp.broadcast_to(window, (blk, 2 * blk)), 0, 1, stride=1, stride_axis=0)
    r_i = lax.broadcasted_iota(jnp.int32, (blk, blk), 0)
    c_i = lax.broadcasted_iota(jnp.int32, (blk, blk), 1)
    on_diag = (delta == 0).astype(jnp.int32)
    o_ref[0, 0] = jnp.where((c_i - r_i) * on_diag <= 0, rolled[:, blk:2 * blk], NEG_BIG)


def _bias_tiles(rev_bias, n_blocks):
    h_ = rev_bias.shape[0]
    blk = MOBA_BLOCK
    return pl.pallas_call(
        functools.partial(_bias_tile_kernel, n_blocks=n_blocks),
        grid=(h_, n_blocks),
        in_specs=[pl.BlockSpec((1, 1, rev_bias.shape[2]), lambda h, dl: (h, 0, 0))],
        out_specs=pl.BlockSpec((1, 1, blk, blk), lambda h, dl: (h, dl, 0, 0)),
        out_shape=jax.ShapeDtypeStruct((h_, n_blocks, blk, blk), F32),
        compiler_params=_cparams("parallel", "parallel"),
        name="moba_bias_tiles",
    )(rev_bias)


def _moba_prompt_kernel(q_ref, k_ref, v_ref, km_ref, bias_ref, o_ref, qa_sc, m_sc, l_sc, acc_sc, *, n_blocks):
    i = pl.program_id(2)
    blk = MOBA_BLOCK
    nb = n_blocks
    q = q_ref[0]
    km1, km2, km3 = _split3(km_ref[0])
    q1, q2, q3 = _split3(q)
    g1 = _dot_nt(jnp.concatenate([km1, km2, km3], axis=0), q1)
    g2 = _dot_nt(jnp.concatenate([km1, km2], axis=0), q2)
    g3 = _dot_nt(km1, q3)
    gates = (g1[2 * nb:3 * nb] + g2[nb:2 * nb] + g3) + (g1[nb:2 * nb] + g2[0:nb]) + g1[0:nb]
    n_i = lax.broadcasted_iota(jnp.int32, (nb, blk), 0)
    n_f = n_i.astype(F32)
    gates = jnp.where(n_i < i, gates, NEG_BIG)
    sel = jnp.where(n_i == i, 1.0, 0.0)
    for _ in range(MOBA_TOPK):
        mx = jnp.max(gates, axis=0, keepdims=True)
        first = jnp.min(jnp.where(gates == mx, n_f, float(nb)), axis=0, keepdims=True)
        pick = (n_f == first) & (mx > 0.5 * NEG_BIG)
        sel = jnp.where(pick, 1.0, sel)
        gates = jnp.where(pick, NEG_BIG, gates)
    hidden = jnp.where(sel > 0.0, 0.0, NEG_BIG)
    hidden = jnp.concatenate([hidden, jnp.zeros((LANES - nb, blk), F32)], axis=0).T
    qa_sc[...] = jnp.concatenate([(q * (MOBA_HEAD_DIM ** -0.5)).astype(BF16), hidden.astype(BF16)], axis=1)
    m_sc[...] = jnp.full(m_sc.shape, NEG_BIG, F32)
    l_sc[...] = jnp.zeros(l_sc.shape, F32)
    acc_sc[...] = jnp.zeros(acc_sc.shape, F32)
    lane_i = lax.broadcasted_iota(jnp.int32, (blk, LANES), 1)

    def logits_of_pair(jj):
        qa = qa_sc[...]
        parts = []
        for j_raw in (2 * jj, 2 * jj + 1):
            j = jnp.minimum(j_raw, nb - 1)
            rows = pl.ds(pl.multiple_of(j * blk, blk), blk)
            k_aug = jnp.concatenate([k_ref[0, rows, :], jnp.where(lane_i == j_raw, 1.0, 0.0).astype(BF16)], axis=1)
            parts.append(_dot_nt(qa, k_aug) + bias_ref[0, jnp.clip(i - j_raw, 0, nb - 1)])
        return jnp.concatenate(parts, axis=1)

    def pair(jj, s_cur):
        s_next = logits_of_pair(jj + 1)
        vs = [v_ref[0, pl.ds(pl.multiple_of(j * blk, blk), blk), :] for j in (2 * jj, 2 * jj + 1)]
        cols = [s_cur[:, x * LANES:(x + 1) * LANES] for x in range(2 * blk // LANES)]
        top = cols[0]
        for x in cols[1:]:
            top = jnp.maximum(top, x)
        m_prev = m_sc[...]
        m_new = jnp.maximum(m_prev, jnp.max(top, axis=-1, keepdims=True))
        ps = [jnp.exp(x - m_new) for x in cols]
        tot = ps[0]
        for x in ps[1:]:
            tot = tot + x
        alpha = jnp.exp(m_prev - m_new)
        l_sc[...] = alpha * l_sc[...] + jnp.sum(tot, axis=-1, keepdims=True)
        p_all = jnp.concatenate(ps, axis=1).astype(BF16)
        acc_sc[...] = alpha * acc_sc[...] + _dot(p_all, jnp.concatenate(vs, axis=0))
        m_sc[...] = m_new
        return s_next

    lax.fori_loop(0, (i + 2) // 2, pair, logits_of_pair(0))
    o_ref[0] = (acc_sc[...] / l_sc[...]).astype(o_ref.dtype)


def _moba_prompt(q, k_bf, v_bf, kmean, bias_tiles):
    b_, t_, _ = q.shape
    nb = t_ // MOBA_BLOCK
    h_ = MOBA_HEADS
    blk = MOBA_BLOCK
    return pl.pallas_call(
        functools.partial(_moba_prompt_kernel, n_blocks=nb),
        grid=(b_, h_, nb),
        in_specs=[
            pl.BlockSpec((1, blk, LANES), lambda bi, h, i: (bi, i, h)),
            pl.BlockSpec((1, t_, LANES), lambda bi, h, i: (bi, 0, h)),
            pl.BlockSpec((1, t_, LANES), lambda bi, h, i: (bi, 0, h)),
            pl.BlockSpec((1, nb, LANES), lambda bi, h, i: (bi, 0, h)),
            pl.BlockSpec((1, nb, blk, blk), lambda bi, h, i: (h, 0, 0, 0)),
        ],
        out_specs=pl.BlockSpec((1, blk, LANES), lambda bi, h, i: (bi, i, h)),
        out_shape=jax.ShapeDtypeStruct((b_, t_, MOBA_WIDTH), BF16),
        scratch_shapes=[pltpu.VMEM((blk, 2 * LANES), BF16), pltpu.VMEM((blk, LANES), F32),
                        pltpu.VMEM((blk, LANES), F32), pltpu.VMEM((blk, LANES), F32)],
        compiler_params=_cparams("parallel", "parallel", "arbitrary"),
        name="moba_prompt",
    )(q, k_bf, v_bf, kmean, bias_tiles)


def _prompt_bias_table(rel_bias, t_):
    y = jnp.arange(t_ + MOBA_BLOCK, dtype=jnp.int32)
    tab = rel_bias[_t5_bucket(t_ - y)]
    tab = jnp.where((y <= t_)[:, None], tab, 0.0)
    return tab.T.reshape(MOBA_HEADS, 1, t_ + MOBA_BLOCK)


def _moba_sample_kernel(pt_ref, q_ref, *refs, n_past):
    n_pages = SAMPLE_BLOCKS_PER_STEP * PAGES_PER_BLOCK
    k_refs, v_refs = refs[:n_pages], refs[n_pages:2 * n_pages]
    kn_ref, vn_ref, bias_ref, ownb_ref, o_ref, m_sc, l_sc, g_sc, acc_sc = refs[2 * n_pages:]
    step = pl.program_id(1)
    scale = MOBA_HEAD_DIM ** -0.5
    rows_n = q_ref.shape[1]
    cols_n = MOBA_BLOCK * MOBA_HEADS
    q = q_ref[0]
    q_bf = (q * scale).astype(BF16)
    shape = (rows_n, LANES)
    for blk in range(SAMPLE_BLOCKS_PER_STEP):
        n = step * SAMPLE_BLOCKS_PER_STEP + blk
        pages = slice(blk * PAGES_PER_BLOCK, (blk + 1) * PAGES_PER_BLOCK)
        kf = jnp.concatenate([r_[...] for r_ in k_refs[pages]], axis=0)
        vf = jnp.concatenate([r_[...] for r_ in v_refs[pages]], axis=0)
        logits = _dot_nt(q_bf, kf.astype(BF16)) + bias_ref[:, pl.ds(pl.multiple_of(n * cols_n, cols_n), cols_n)]
        m_b = jnp.max(logits, axis=-1, keepdims=True)
        p = jnp.exp(logits - m_b)
        m_sc[n] = jnp.broadcast_to(m_b, shape)
        l_sc[n] = jnp.broadcast_to(jnp.sum(p, axis=-1, keepdims=True), shape)
        acc_sc[n] = _dot(p.astype(BF16), vf.astype(BF16))
        kmean = jnp.mean(kf.reshape(MOBA_BLOCK, MOBA_HEADS, MOBA_HEAD_DIM), axis=0)
        gate = jnp.concatenate([jnp.sum(q[t * MOBA_HEADS:(t + 1) * MOBA_HEADS] * kmean, axis=-1, keepdims=True)
                                for t in range(rows_n // MOBA_HEADS)], axis=0)
        g_sc[n] = jnp.broadcast_to(gate, shape)

    @pl.when(step == pl.num_programs(1) - 1)
    def _():
        lane_f = lax.broadcasted_iota(jnp.int32, (1, LANES), 1).astype(F32)
        gates = jnp.full(shape, NEG_BIG, F32)
        for nb in range(n_past):
            gates = jnp.where(lane_f == float(nb), g_sc[nb], gates)
        sel = _top_blocks(gates, lane_f)
        s_own = _dot_nt(q_bf, kn_ref[0].astype(BF16)) + ownb_ref[...]
        m_tot = jnp.broadcast_to(jnp.max(s_own, axis=-1, keepdims=True), shape)
        picked = []
        for nb in range(n_past):
            on = jnp.max(jnp.where(lane_f == float(nb), sel, 0.0), axis=-1, keepdims=True) > 0.0
            picked.append(on)
            m_tot = jnp.maximum(m_tot, jnp.where(on, m_sc[nb], NEG_BIG))
        p_own = jnp.exp(s_own - m_tot[:, 0:rows_n])
        l_tot = jnp.broadcast_to(jnp.sum(p_own, axis=-1, keepdims=True), shape)
        acc = _dot(p_own.astype(BF16), vn_ref[0].astype(BF16))
        for nb in range(n_past):
            w = jnp.where(picked[nb], jnp.exp(m_sc[nb] - m_tot), 0.0)
            l_tot = l_tot + w * l_sc[nb]
            acc = acc + w * acc_sc[nb]
        o_ref[0] = (acc / l_tot).astype(o_ref.dtype)


def _moba_sample(layer, page_table, q, k_new, v_new, cache_k, cache_v, bias_tab, own_bias):
    s_, rows_n, _ = q.shape
    n_past = page_table.shape[1] * PAGE_SIZE // MOBA_BLOCK
    row = pl.BlockSpec((1, rows_n, MOBA_HEAD_DIM), lambda si, n, pt: (si, 0, 0))

    pages_per_step = SAMPLE_BLOCKS_PER_STEP * PAGES_PER_BLOCK

    def page_spec(pg):
        return pl.BlockSpec((None, None, PAGE_SIZE * MOBA_HEADS, MOBA_HEAD_DIM),
                            lambda si, n, pt: (layer, pt[si, pages_per_step * n + pg], 0, 0))

    tiles = [page_spec(pg) for pg in range(pages_per_step)]
    rows_shape = cache_k.shape[:2] + (PAGE_SIZE * MOBA_HEADS, MOBA_HEAD_DIM)
    cache_k = cache_k.reshape(rows_shape)
    cache_v = cache_v.reshape(rows_shape)
    grid_spec = pltpu.PrefetchScalarGridSpec(
        num_scalar_prefetch=1,
        grid=(s_, n_past // SAMPLE_BLOCKS_PER_STEP),
        in_specs=[row] + tiles + tiles + [
            row,
            row,
            pl.BlockSpec(bias_tab.shape, lambda si, n, pt: (0, 0)),
            pl.BlockSpec(own_bias.shape, lambda si, n, pt: (0, 0)),
        ],
        out_specs=row,
        scratch_shapes=[pltpu.VMEM((n_past, rows_n, LANES), F32) for _ in range(4)],
    )
    return pl.pallas_call(
        functools.partial(_moba_sample_kernel, n_past=n_past),
        grid_spec=grid_spec,
        out_shape=jax.ShapeDtypeStruct((s_, rows_n, MOBA_HEAD_DIM), BF16),
        compiler_params=_cparams("parallel", "arbitrary"),
        name="moba_sample",
    )(page_table, q, *([cache_k] * len(tiles)), *([cache_v] * len(tiles)), k_new, v_new, bias_tab, own_bias)


def _sample_bias_tables(rel_bias, past_len, t_new):
    h_ = rel_bias.shape[1]
    same_head = jnp.eye(h_, dtype=bool)
    dist = past_len + t_new - 1 - jnp.arange(past_len + t_new, dtype=jnp.int32)
    by_dist_rev = rel_bias[_t5_bucket(dist)]
    tab = jnp.stack([by_dist_rev[t_new - 1 - t:t_new - 1 - t + past_len] for t in range(t_new)])
    per_key = jnp.transpose(tab, (0, 2, 1)).reshape(t_new * h_, past_len)
    t = jnp.arange(t_new, dtype=jnp.int32)
    own = rel_bias[_t5_bucket(t[:, None] - t[None, :])]
    own = jnp.where((t[None, :] <= t[:, None])[..., None], own, NEG_BIG)
    own = jnp.where(same_head[None, :, None, :], jnp.transpose(own, (0, 2, 1))[..., None], NEG_BIG)
    return _spread_over_heads(per_key, h_), own.reshape(t_new * h_, t_new * h_)


def _spread_kernel(x_ref, e_ref, neg_ref, o_ref):
    hi, mid, lo = _split3(x_ref[...])
    e = e_ref[...]
    o_ref[...] = (_dot(hi, e) + _dot(mid, e) + _dot(lo, e)) + neg_ref[...]


def _spread_over_heads(per_key, h_):
    rows_n, n_keys = per_key.shape
    blk = MOBA_BLOCK
    spread = np.zeros((blk, blk * h_), np.float32)
    spread[np.arange(blk * h_) // h_, np.arange(blk * h_)] = 1.0
    other_head = np.where((np.arange(rows_n) % h_)[:, None] == (np.arange(blk * h_) % h_)[None, :], 0.0, NEG_BIG)
    return pl.pallas_call(
        _spread_kernel,
        grid=(n_keys // blk,),
        in_specs=[
            pl.BlockSpec((rows_n, blk), lambda n: (0, n)),
            pl.BlockSpec(spread.shape, lambda n: (0, 0)),
            pl.BlockSpec(other_head.shape, lambda n: (0, 0)),
        ],
        out_specs=pl.BlockSpec((rows_n, blk * h_), lambda n: (0, n)),
        out_shape=jax.ShapeDtypeStruct((rows_n, n_keys * h_), F32),
        compiler_params=_cparams("parallel"),
        name="moba_sample_bias",
    )(per_key, jnp.asarray(spread, BF16), jnp.asarray(other_head, F32))


PROMPT_ROW_TILE = 1024

SH1, SC1, G1, SH2, SC2, G2 = range(6)


def _layer_weights(l, w_in, q_norm, k_norm, w_up_a, w_up_b, w_out, w_ffn_gate, w_ffn_up, w_ffn_down):
    o_q = RWKV_PROJ
    o_v = o_q + 2 * MOBA_WIDTH
    o_ga = o_v + MOBA_WIDTH
    o_gb = o_ga + D_MODEL
    wl = w_in[l]
    heads = MOBA_WIDTH // MOBA_HEAD_DIM
    return {
        "rwkv": jnp.pad(wl[:, :RWKV_PROJ], ((0, 0), (0, RWKV_PROJ_PAD - RWKV_PROJ))).astype(BF16),
        "q": wl[:, o_q:o_q + MOBA_WIDTH].astype(BF16),
        "k": wl[:, o_q + MOBA_WIDTH:o_v].astype(BF16),
        "v": wl[:, o_v:o_ga].astype(BF16),
        "ga": wl[:, o_ga:o_gb].astype(BF16),
        "gb": wl[:, o_gb:].astype(BF16),
        "q_gain": jnp.tile(q_norm[l], heads)[None, :],
        "k_gain": jnp.tile(k_norm[l], heads)[None, :],
        "up_a": (w_up_a, l),
        "up_b": (w_up_b, l),
        "out": w_out[l].astype(BF16),
        "fg": (w_ffn_gate, l),
        "fu": (w_ffn_up, l),
        "fd": (w_ffn_down, l),
    }


def _trunk_layer(x, mod, norm_mix, norm_ffn, w, rwkv_fn, attn_fn, *, tm):
    h = _norm_mod(x, norm_mix, mod, SH1, SC1, tt=min(tm, 512))
    z_r = _matmul("in_rwkv", [h], [w["rwkv"]], [0], _ep_plain, [F32], tm=tm, tn=RWKV_PROJ_PAD // 3)[0]
    q = _matmul("in_q", [h], [w["q"]], [0], _ep_head_norm, [F32], tm=tm, tn=1024, cols=[w["q_gain"]])[0]
    k, k_bf = _matmul("in_k", [h], [w["k"]], [0], _ep_head_norm, [F32, BF16], tm=tm, tn=1024, cols=[w["k_gain"]])
    v, v_bf = _matmul("in_v", [h], [w["v"]], [0], _ep_plain2, [F32, BF16], tm=tm, tn=1024)
    y_a, s_new, shift_new = rwkv_fn(z_r)
    y_b = attn_fn(q, k, k_bf, v, v_bf)
    merged = _matmul("merge", [h, y_a, y_b], [w["ga"], w["gb"], w["up_a"], w["up_b"]], [0, 0, 1, 2],
                     _ep_merge, [BF16], tm=tm, tn=512)[0]
    x1 = _matmul("out_proj", [merged], [w["out"]], [0], _ep_residual, [F32], tm=min(2 * tm, x.shape[1]), tn=512,
                 tiles=[x], mods=[(mod, G1)])[0]
    h2 = _norm_mod(x1, norm_ffn, mod, SH2, SC2, tt=min(tm, 512))
    act = _matmul("ffn_in", [h2], [w["fg"], w["fu"]], [0, 0], _ep_swiglu, [BF16], tm=tm, tn=512)[0]
    x2 = _matmul("ffn_out", [act], [w["fd"]], [0], _ep_residual, [F32], tm=tm, tn=256,
                 tiles=[x1], mods=[(mod, G2)])[0]
    return x2, k, v, s_new, shift_new


def kernel(x_prompt, x_sample, cache_k, cache_v, state_wkv, state_shift, page_table, c_prompt, c_sample, rel_bias, w_ada, b_ada, norm_mix, norm_ffn, w_in, mu_shift, w0, w_lora, a0, a_lora, g_lora, k_k, k_a, r_k, lnx_w, lnx_b, q_norm, k_norm, w_up_a, w_up_b, w_out, w_ffn_gate, w_ffn_up, w_ffn_down):
    bp, tp, d = x_prompt.shape
    bs, ts, _ = x_sample.shape
    n_seq = bp + bs
    c_all = jnp.concatenate([c_prompt, c_sample])
    c_all = jnp.pad(c_all, ((0, -n_seq % SUBLANES), (0, 0)))
    mod = _ada_mod(c_all, w_ada, b_ada)

    past_len = page_table.shape[1] * PAGE_SIZE
    bias_tiles = _bias_tiles(_prompt_bias_table(rel_bias, tp), tp // MOBA_BLOCK)
    bias_tab, own_bias = _sample_bias_tables(rel_bias, past_len, ts)
    pr_pad = RWKV_PROJ_PAD - RWKV_PROJ
    sample_chunk = SUBLANES

    def sample_rows(a):
        return a.reshape(bs, ts * MOBA_HEADS, MOBA_HEAD_DIM)

    x_p = x_prompt
    x_s = x_sample.reshape(1, bs * ts, d)
    outs = [[] for _ in range(8)]
    for l in range(DEPTH):
        w = _layer_weights(l, w_in, q_norm, k_norm, w_up_a, w_up_b, w_out, w_ffn_gate, w_ffn_up, w_ffn_down)
        lw = _rwkv_weights(l, mu_shift, w0, w_lora, a0, a_lora, g_lora, k_k, k_a, r_k, lnx_w, lnx_b)
        mod_p = mod[l, :bp][:, None, :]
        mod_s = jnp.repeat(mod[l, bp:n_seq], ts, axis=0)[None]

        def rwkv_p(z_r):
            shift0 = jnp.zeros((bp, 1, RWKV_PROJ_PAD), F32)
            s0 = jnp.zeros((bp, RWKV_HEADS, RWKV_HEAD_DIM, RWKV_HEAD_DIM), F32)
            return _rwkv_branch(z_r, shift0, s0, lw, chunk=64, t_valid=tp)

        def attn_p(q, k, k_bf, v, v_bf):
            return _moba_prompt(q, k_bf, v_bf, _block_means(k), bias_tiles)

        x_p, k_p, v_p, s_p, sh_p = _trunk_layer(x_p, mod_p, norm_mix[l], norm_ffn[l], w, rwkv_p, attn_p,
                                                 tm=min(tp, PROMPT_ROW_TILE))

        def rwkv_s(z_r):
            z = z_r.reshape(bs, ts, RWKV_PROJ_PAD)
            z = jnp.pad(z, ((0, 0), (0, sample_chunk - ts), (0, 0)))
            shift0 = jnp.pad(state_shift[l], ((0, 0), (0, pr_pad)))[:, None, :]
            y, s_new, shift_new = _rwkv_branch(z, shift0, state_wkv, lw, chunk=sample_chunk, t_valid=ts,
                                               s0_layer=l)
            return y[:, :ts].reshape(1, bs * ts, RWKV_WIDTH), s_new, shift_new

        def attn_s(q, k, k_bf, v, v_bf):
            y = _moba_sample(l, page_table, sample_rows(q), sample_rows(k), sample_rows(v), cache_k, cache_v,
                             bias_tab, own_bias)
            return y.reshape(1, bs * ts, MOBA_WIDTH)

        x_s, k_s, v_s, s_s, sh_s = _trunk_layer(x_s, mod_s, norm_mix[l], norm_ffn[l], w, rwkv_s, attn_s,
                                                 tm=bs * ts)

        kv_shape_p = (bp, tp, MOBA_HEADS, MOBA_HEAD_DIM)
        kv_shape_s = (bs, ts, MOBA_HEADS, MOBA_HEAD_DIM)
        for lst, val in zip(outs, [
                k_p.reshape(kv_shape_p), v_p.reshape(kv_shape_p),
                s_p, sh_p[:, 0, :RWKV_PROJ],
                k_s.reshape(kv_shape_s), v_s.reshape(kv_shape_s),
                s_s, sh_s[:, 0, :RWKV_PROJ]]):
            lst.append(val)
    return (x_p, x_s.reshape(bs, ts, d)) + tuple(jnp.stack(o) for o in outs)
```
